```python
import math
import jax, jax.numpy as jnp
from jax import lax
import numpy as np

D_MODEL = 1024
BATCH = 32
SEQ = 2048
DEPTH = 4

N_MIXERS = 3
D_FF = 4 * D_MODEL
RMS_EPS = 1e-6
NEG_INF = -1e30
NUM_BUCKETS = 32
REL_MAX_DISTANCE = 2048
DA_HEAD_DIM = 64
DA_HEADS = D_MODEL // (2 * DA_HEAD_DIM)
DA_Q_BLOCK = 128
LRU_WIDTH = D_MODEL
LRU_BLOCK_WIDTH = 256
LRU_BLOCKS = LRU_WIDTH // LRU_BLOCK_WIDTH
CONV_WIDTH = 4
LRU_C = 8.0
DIL_HEAD_DIM = 64
DIL_HEADS = D_MODEL // DIL_HEAD_DIM
DIL_PATTERNS = ((128, 1), (512, 4), (2048, 16))
DIL_GROUPS = len(DIL_PATTERNS)
BAND_BLOCK = 128
N_BIAS_COLS = 2 * DA_HEADS

kernel_name = 'hybrid_diffattn_rglru_dilated_adaln'


def rms_norm(x, gain):
    xf = x.astype(jnp.float32)
    y = xf * lax.rsqrt(jnp.mean(xf * xf, axis=-1, keepdims=True) + RMS_EPS)
    return (y * gain.astype(jnp.float32)).astype(x.dtype)


def modulate(h, shift, scale):
    return h * (1.0 + scale[:, None, :]) + shift[:, None, :]


def t5_bucket(dist):
    n = jnp.maximum(dist, 0)
    max_exact = NUM_BUCKETS // 2
    nf = jnp.maximum(n, max_exact).astype(jnp.float32)
    large = max_exact + (jnp.log(nf / max_exact) / math.log(REL_MAX_DISTANCE / max_exact)
                         * (NUM_BUCKETS - max_exact)).astype(jnp.int32)
    large = jnp.minimum(large, NUM_BUCKETS - 1)
    return jnp.where(n < max_exact, n, large)


def differential_attention(h, w_qkv, w_o, q_gain, k_gain, lam_q1, lam_k1, lam_q2, lam_k2,
                           sub_gain, rel_bias, lambda_init):
    B, S, _ = h.shape
    q, k, v = jnp.split(h @ w_qkv, 3, axis=-1)
    q = rms_norm(q.reshape(B, S, DA_HEADS, 2, DA_HEAD_DIM), q_gain) * (DA_HEAD_DIM ** -0.5)
    k = rms_norm(k.reshape(B, S, DA_HEADS, 2, DA_HEAD_DIM), k_gain)
    v = v.reshape(B, S, DA_HEADS, 2 * DA_HEAD_DIM)
    lam = (jnp.exp(jnp.sum(lam_q1.astype(jnp.float32) * lam_k1.astype(jnp.float32)))
           - jnp.exp(jnp.sum(lam_q2.astype(jnp.float32) * lam_k2.astype(jnp.float32)))
           + lambda_init)
    bq = min(DA_Q_BLOCK, S)
    outs = []
    for j in range(S // bq):
        q0, kv_len = j * bq, (j + 1) * bq
        s = jnp.einsum('bqhmd,bkhmd->bhmqk', q[:, q0:kv_len], k[:, :kv_len]).astype(jnp.float32)
        dist = jnp.arange(q0, kv_len)[:, None] - jnp.arange(kv_len)[None, :]
        bias = rel_bias[t5_bucket(dist)].reshape(bq, kv_len, DA_HEADS, 2).transpose(2, 3, 0, 1)
        s = jnp.where(dist >= 0, s + bias.astype(jnp.float32), NEG_INF)
        p = jax.nn.softmax(s, axis=-1)
        p = p[:, :, 0] - lam * p[:, :, 1]
        outs.append(jnp.einsum('bhqk,bkhe->bqhe', p.astype(v.dtype), v[:, :kv_len]))
    o = jnp.concatenate(outs, axis=1)
    o = rms_norm(o, sub_gain) * (1.0 - lambda_init)
    return o.reshape(B, S, D_MODEL) @ w_o


def rglru_block(h, w_in, conv_w, conv_b, w_x, b_x, w_a, b_a, a_param, w_out):
    B, S, _ = h.shape
    y, u = jnp.split(h @ w_in, 2, axis=-1)
    u_pad = jnp.pad(u, ((0, 0), (CONV_WIDTH - 1, 0), (0, 0)))
    conv = conv_b
    for tap in range(CONV_WIDTH):
        start = CONV_WIDTH - 1 - tap
        conv = conv + u_pad[:, start:start + S] * conv_w[tap]
    u = conv
    ub = u.reshape(B, S, LRU_BLOCKS, LRU_BLOCK_WIDTH)
    gate_x = jax.nn.sigmoid((jnp.einsum('bsni,nij->bsnj', ub, w_x).reshape(B, S, LRU_WIDTH) + b_x).astype(jnp.float32))
    gate_a = jax.nn.sigmoid((jnp.einsum('bsni,nij->bsnj', ub, w_a).reshape(B, S, LRU_WIDTH) + b_a).astype(jnp.float32))
    log_a = -LRU_C * gate_a * jax.nn.softplus(-a_param.astype(jnp.float32))
    a = jnp.exp(log_a)
    b = jnp.sqrt(-jnp.expm1(2.0 * log_a)) * (gate_x * u.astype(jnp.float32))

    def combine(left, right):
        a_l, b_l = left
        a_r, b_r = right
        return a_l * a_r, a_r * b_l + b_r

    _, hs = lax.associative_scan(combine, (a, b), axis=1)
    out = hs.astype(h.dtype) * jax.nn.gelu(y, approximate=True)
    return out @ w_out


def dilated_branch(q, k, v, dilation, sub_window, rel_bias):
    B, S, H, d = q.shape
    L = S // dilation
    bq = min(BAND_BLOCK, L)
    nb = -(-L // bq)
    pad = nb * bq - L

    def to_blocks(t):
        t = t.reshape(B, L, dilation, H, d)
        t = jnp.pad(t, ((0, 0), (0, pad), (0, 0), (0, 0), (0, 0)))
        t = t.reshape(B, nb, bq, dilation, H, d).transpose(1, 0, 3, 4, 2, 5)
        return t.reshape(nb, B * dilation, H, bq, d)

    def with_prev(t):
        prev = jnp.concatenate([jnp.zeros_like(t[:1]), t[:-1]], axis=0)
        return jnp.concatenate([prev, t], axis=3)

    qb = to_blocks(q)
    kb = with_prev(to_blocks(k))
    vb = with_prev(to_blocks(v))
    rel = jnp.arange(bq)[:, None] + bq - jnp.arange(2 * bq)[None, :]
    band = (rel >= 0) & (rel <= sub_window)
    bias = rel_bias[t5_bucket(rel * dilation)].transpose(2, 0, 1).astype(jnp.float32)
    in_current = jnp.arange(2 * bq) >= bq

    def block_step(args):
        j, qj, kj, vj = args
        s = jnp.einsum('nhqd,nhkd->nhqk', qj, kj).astype(jnp.float32) + bias
        mask = band & ((j > 0) | in_current)[None, :]
        s = jnp.where(mask, s, NEG_INF)
        m = jnp.max(s, axis=-1, keepdims=True)
        p = jnp.exp(s - m)
        den = jnp.sum(p, axis=-1)
        o = jnp.einsum('nhqk,nhkd->nhqd', (p / den[..., None]).astype(vj.dtype), vj).astype(jnp.float32)
        return o, m[..., 0] + jnp.log(den)

    o, lse = lax.map(block_step, (jnp.arange(nb), qb, kb, vb))
    o = o.reshape(nb, B, dilation, H, bq, d).transpose(1, 0, 4, 2, 3, 5)
    o = o.reshape(B, nb * bq, dilation, H, d)[:, :L].reshape(B, S, H, d)
    lse = lse.reshape(nb, B, dilation, H, bq).transpose(1, 0, 4, 2, 3)
    lse = lse.reshape(B, nb * bq, dilation, H)[:, :L].reshape(B, S, H)
    return o, lse


def dilated_attention(h, w_qkv, w_o, q_gain, k_gain, rel_bias):
    B, S, _ = h.shape
    o_acc, lse_acc = None, None
    for g, (window, dilation) in enumerate(DIL_PATTERNS):
        cols = w_qkv[:, g * 3 * D_MODEL:(g + 1) * 3 * D_MODEL]
        q, k, v = jnp.split(h @ cols, 3, axis=-1)
        q = rms_norm(q.reshape(B, S, DIL_HEADS, DIL_HEAD_DIM), q_gain) * (DIL_HEAD_DIM ** -0.5)
        k = rms_norm(k.reshape(B, S, DIL_HEADS, DIL_HEAD_DIM), k_gain)
        v = v.reshape(B, S, DIL_HEADS, DIL_HEAD_DIM)
        o_g, lse_g = dilated_branch(q, k, v, dilation, window // dilation, rel_bias)
        if o_acc is None:
            o_acc, lse_acc = o_g, lse_g
        else:
            lse_new = jnp.logaddexp(lse_acc, lse_g)
            o_acc = (o_acc * jnp.exp(lse_acc - lse_new)[..., None]
                     + o_g * jnp.exp(lse_g - lse_new)[..., None])
            lse_acc = lse_new
    return o_acc.astype(h.dtype).reshape(B, S, D_MODEL) @ w_o


def squared_relu_mlp(h, w1, w2):
    return jnp.square(jax.nn.relu(h @ w1)) @ w2


def setup_inputs(seed: int = 0) -> dict:
    key = jax.random.key(seed)
    ks = iter(jax.random.split(key, 40))
    n_a = len(range(0, DEPTH, N_MIXERS))
    n_b = len(range(1, DEPTH, N_MIXERS))
    n_c = len(range(2, DEPTH, N_MIXERS))
    D, W = D_MODEL, LRU_WIDTH

    def nrm(shape, scale):
        return jax.random.normal(next(ks), shape, jnp.float32) * scale

    def gain(shape):
        return 1.0 + nrm(shape, 0.05)

    inp = {}
    inp['x'] = nrm((BATCH, SEQ, D), 1.0)
    inp['c'] = nrm((BATCH, D), 1.0)
    inp['rel_bias'] = nrm((NUM_BUCKETS, N_BIAS_COLS), 0.5)
    inp['w_ada'] = nrm((DEPTH, D, 6 * D), 0.5 * D ** -0.5)
    inp['b_ada'] = nrm((DEPTH, 6 * D), 0.02)
    inp['norm_mix'] = gain((DEPTH, D))
    inp['norm_mlp'] = gain((DEPTH, D))
    inp['mlp_w1'] = nrm((DEPTH, D, D_FF), D ** -0.5)
    inp['mlp_w2'] = nrm((DEPTH, D_FF, D), D_FF ** -0.5)
    inp['da_w_qkv'] = nrm((n_a, D, 3 * D), D ** -0.5)
    inp['da_w_o'] = nrm((n_a, D, D), D ** -0.5)
    inp['da_q_gain'] = gain((n_a, DA_HEAD_DIM))
    inp['da_k_gain'] = gain((n_a, DA_HEAD_DIM))
    inp['da_lam_q1'] = nrm((n_a, DA_HEAD_DIM), 0.1)
    inp['da_lam_k1'] = nrm((n_a, DA_HEAD_DIM), 0.1)
    inp['da_lam_q2'] = nrm((n_a, DA_HEAD_DIM), 0.1)
    inp['da_lam_k2'] = nrm((n_a, DA_HEAD_DIM), 0.1)
    inp['da_sub_gain'] = gain((n_a, 2 * DA_HEAD_DIM))
    inp['lru_w_in'] = nrm((n_b, D, 2 * W), D ** -0.5)
    inp['lru_conv_w'] = nrm((n_b, CONV_WIDTH, W), CONV_WIDTH ** -0.5)
    inp['lru_conv_b'] = nrm((n_b, W), 0.02)
    inp['lru_w_x'] = nrm((n_b, LRU_BLOCKS, LRU_BLOCK_WIDTH, LRU_BLOCK_WIDTH), LRU_BLOCK_WIDTH ** -0.5)
    inp['lru_b_x'] = nrm((n_b, W), 0.02)
    inp['lru_w_a'] = nrm((n_b, LRU_BLOCKS, LRU_BLOCK_WIDTH, LRU_BLOCK_WIDTH), LRU_BLOCK_WIDTH ** -0.5)
    inp['lru_b_a'] = nrm((n_b, W), 0.02)
    a_c = jax.random.uniform(next(ks), (n_b, W), jnp.float32, minval=0.9, maxval=0.999)
    a0 = a_c ** (1.0 / LRU_C)
    inp['lru_a_param'] = jnp.log(a0) - jnp.log1p(-a0)
    inp['lru_w_out'] = nrm((n_b, W, D), W ** -0.5)
    inp['dil_w_qkv'] = nrm((n_c, D, DIL_GROUPS * 3 * D), D ** -0.5)
    inp['dil_w_o'] = nrm((n_c, D, D), D ** -0.5)
    inp['dil_q_gain'] = gain((n_c, DIL_HEAD_DIM))
    inp['dil_k_gain'] = gain((n_c, DIL_HEAD_DIM))
    return inp


def reference(x, c, rel_bias, w_ada, b_ada, norm_mix, norm_mlp, mlp_w1, mlp_w2,
              da_w_qkv, da_w_o, da_q_gain, da_k_gain, da_lam_q1, da_lam_k1, da_lam_q2, da_lam_k2, da_sub_gain,
              lru_w_in, lru_conv_w, lru_conv_b, lru_w_x, lru_b_x, lru_w_a, lru_b_a, lru_a_param, lru_w_out,
              dil_w_qkv, dil_w_o, dil_q_gain, dil_k_gain):
    mod = jnp.einsum('bd,ldk->lbk', jax.nn.silu(c), w_ada) + b_ada[:, None, :]
    for i in range(DEPTH):
        shift1, scale1, gate1, shift2, scale2, gate2 = jnp.split(mod[i], 6, axis=-1)
        h = modulate(rms_norm(x, norm_mix[i]), shift1, scale1)
        kind, slot = i % N_MIXERS, i // N_MIXERS
        if kind == 0:
            lambda_init = 0.8 - 0.6 * math.exp(-0.3 * i)
            y = differential_attention(h, da_w_qkv[slot], da_w_o[slot], da_q_gain[slot], da_k_gain[slot],
                                       da_lam_q1[slot], da_lam_k1[slot], da_lam_q2[slot], da_lam_k2[slot],
                                       da_sub_gain[slot], rel_bias, lambda_init)
        elif kind == 1:
            y = rglru_block(h, lru_w_in[slot], lru_conv_w[slot], lru_conv_b[slot], lru_w_x[slot], lru_b_x[slot],
                            lru_w_a[slot], lru_b_a[slot], lru_a_param[slot], lru_w_out[slot])
        else:
            y = dilated_attention(h, dil_w_qkv[slot], dil_w_o[slot], dil_q_gain[slot], dil_k_gain[slot], rel_bias)
        x = x + gate1[:, None, :] * y
        h = modulate(rms_norm(x, norm_mlp[i]), shift2, scale2)
        x = x + gate2[:, None, :] * squared_relu_mlp(h, mlp_w1[i], mlp_w2[i])
    return x
```

```python
import functools
import math

import numpy as np
import jax
import jax.numpy as jnp
from jax import lax
from jax.experimental import pallas as pl
from jax.experimental.pallas import tpu as pltpu

F32 = jnp.float32
BF16 = jnp.bfloat16

RMS_EPS = 1e-6
NEG_INF = -1e30
NUM_BUCKETS = 32
REL_MAX_DISTANCE = 2048
HEAD_DIM = 64
N_MIXERS = 3
CONV_WIDTH = 4
LRU_C = 8.0
DIL_PATTERNS = ((128, 1), (512, 4), (2048, 16))
MASKED_BUCKET = NUM_BUCKETS
BUCKET_PAD = 64

LANES = 128
SUBLANES = 8
MXU_WIDTH = 256
VMEM_LIMIT_BYTES = 56 * 1024 * 1024

TQ = MXU_WIDTH


def _params(*sem):
    return pltpu.CompilerParams(dimension_semantics=sem, vmem_limit_bytes=VMEM_LIMIT_BYTES)


def _resident(shape):
    nd = len(shape)
    return pl.BlockSpec(shape, lambda *_: (0,) * nd, pipeline_mode=pl.Buffered(1))


def _norm_mod(x, gain, shift, scale):
    ms = jnp.mean(x * x, axis=-1, keepdims=True)
    y = (x * lax.rsqrt(ms + RMS_EPS)) * gain
    return y * (1.0 + scale) + shift


def _ada_kernel(c_ref, w_ref, b_ref, o_ref):
    c = c_ref[...]
    a = (c * jax.nn.sigmoid(c)).astype(BF16)
    o_ref[0] = jnp.dot(a, w_ref[0].astype(BF16), preferred_element_type=F32) + b_ref[0]


def _ada(c, w_ada, b_ada):
    depth, d, n = w_ada.shape
    bsz = c.shape[0]
    tn = n // 4
    return pl.pallas_call(
        _ada_kernel,
        out_shape=jax.ShapeDtypeStruct((depth, bsz, n), F32),
        grid=(depth, n // tn),
        in_specs=[pl.BlockSpec((bsz, d), lambda l, j: (0, 0)),
                  pl.BlockSpec((1, d, tn), lambda l, j: (l, 0, j)),
                  pl.BlockSpec((1, 1, tn), lambda l, j: (l, 0, j))],
        out_specs=pl.BlockSpec((1, bsz, tn), lambda l, j: (l, 0, j)),
        compiler_params=_params("arbitrary", "arbitrary"),
        name="ada_mod",
    )(c, w_ada, b_ada.reshape(depth, 1, n))


def _t5_bucket_np(n):
    n = np.maximum(n, 0)
    max_exact = NUM_BUCKETS // 2
    nf = np.maximum(n, max_exact).astype(np.float32)
    large = max_exact + (np.log(nf / np.float32(max_exact)) / np.float32(math.log(REL_MAX_DISTANCE / max_exact))
                         * np.float32(NUM_BUCKETS - max_exact)).astype(np.int32)
    large = np.minimum(large, NUM_BUCKETS - 1)
    return np.where(n < max_exact, n, large)


def _bucket_tiles(nd, dilation, window):
    key = np.arange(TQ)[:, None]
    qry = np.arange(TQ)[None, :]
    tiles = []
    for delta in range(nd):
        dist = TQ * delta + qry - key
        valid = dist >= 0
        if dilation is not None:
            valid &= (dist % dilation == 0) & (dist <= window)
        bucket = _t5_bucket_np(dist)
        tiles.append(np.where(valid, bucket, MASKED_BUCKET).reshape(1, TQ * TQ))
    return np.stack(tiles).astype(np.int32)


def _bias_kernel(bk_ref, rb_ref, o_ref):
    rb = rb_ref[...]
    hi = rb.astype(BF16)
    r1 = rb - hi.astype(F32)
    mid = r1.astype(BF16)
    low = (r1 - mid.astype(F32)).astype(BF16)
    n = o_ref.shape[2]
    chunk = 8192
    for c in range(n // chunk):
        bk = bk_ref[0, :, c * chunk:(c + 1) * chunk]
        onehot = lax.broadcasted_iota(jnp.int32, (BUCKET_PAD, chunk), 0) == bk
        oh = jnp.where(onehot, 1.0, 0.0).astype(BF16)
        o_ref[0, :, c * chunk:(c + 1) * chunk] = (
            jnp.dot(hi, oh, preferred_element_type=F32)
            + jnp.dot(mid, oh, preferred_element_type=F32)
            + jnp.dot(low, oh, preferred_element_type=F32))


def _bias_tiles(rb_ext, nd, dilation, window):
    buckets = jnp.asarray(_bucket_tiles(nd, dilation, window))
    ncols = rb_ext.shape[0]
    out = pl.pallas_call(
        _bias_kernel,
        out_shape=jax.ShapeDtypeStruct((nd, ncols, TQ * TQ), F32),
        grid=(nd,),
        in_specs=[pl.BlockSpec((1, 1, TQ * TQ), lambda i: (i, 0, 0)),
                  pl.BlockSpec((ncols, BUCKET_PAD), lambda i: (0, 0))],
        out_specs=pl.BlockSpec((1, ncols, TQ * TQ), lambda i: (i, 0, 0)),
        compiler_params=_params("arbitrary"),
        name="rel_bias_tiles",
    )(buckets, rb_ext)
    return out.reshape(nd, ncols, TQ, TQ)


def _qkv_kernel(x_ref, g_ref, sh_ref, sc_ref, wq_ref, wk_ref, wv_ref, qg_ref, kg_ref,
                qt_ref, k_ref, vt_ref):
    x = x_ref[...]
    tm, d = x.shape
    groups = wq_ref.shape[0]
    h = _norm_mod(x, g_ref[...], sh_ref[0], sc_ref[0]).astype(BF16)
    nt = (((1,), (1,)), ((), ()))
    lo = lax.broadcasted_iota(jnp.int32, (tm, LANES), 1) < HEAD_DIM
    nsub = tm // TQ
    for g in range(groups):
        yq = lax.dot_general(wq_ref[g], h, nt, preferred_element_type=F32)
        y3 = yq.reshape(d // HEAD_DIM, HEAD_DIM, tm)
        ms = jnp.mean(y3 * y3, axis=1, keepdims=True)
        y3 = (y3 * lax.rsqrt(ms + RMS_EPS)) * qg_ref[...][None]
        yb = y3.reshape(d // LANES, LANES, tm).astype(BF16)
        for t in range(nsub):
            qt_ref[g, :, t] = yb[:, :, t * TQ:(t + 1) * TQ]
        yk = jnp.dot(h, wk_ref[g], preferred_element_type=F32)
        for c in range(d // LANES):
            yc = yk[:, c * LANES:(c + 1) * LANES]
            y2 = yc * yc
            s_lo = jnp.sum(jnp.where(lo, y2, 0.0), axis=-1, keepdims=True)
            s_hi = jnp.sum(jnp.where(lo, 0.0, y2), axis=-1, keepdims=True)
            ms = jnp.where(lo, s_lo, s_hi) * (1.0 / HEAD_DIM)
            k_ref[g, :, c * LANES:(c + 1) * LANES] = (
                (yc * lax.rsqrt(ms + RMS_EPS)) * kg_ref[...]).astype(BF16)
        yv = lax.dot_general(wv_ref[g], h, nt, preferred_element_type=F32)
        yvb = yv.reshape(d // LANES, LANES, tm).astype(BF16)
        for t in range(nsub):
            vt_ref[g, :, t] = yvb[:, :, t * TQ:(t + 1) * TQ]


def _qkv(x, mod, layer, gain, wq, wk, wv, qg, kg, *, seq, tm):
    t, d = x.shape
    groups = wq.shape[0]
    bsz = t // seq
    nblk = d // LANES
    row = lambda i: layer * bsz + (i * tm) // seq
    tshape = (groups, nblk, t // TQ, LANES, TQ)
    return pl.pallas_call(
        _qkv_kernel,
        out_shape=(jax.ShapeDtypeStruct(tshape, BF16),
                   jax.ShapeDtypeStruct((groups, t, d), BF16),
                   jax.ShapeDtypeStruct(tshape, BF16)),
        grid=(t // tm,),
        in_specs=[pl.BlockSpec((tm, d), lambda i: (i, 0)),
                  pl.BlockSpec((1, d), lambda i: (0, 0)),
                  pl.BlockSpec((1, 1, d), lambda i: (row(i), 0, 0)),
                  pl.BlockSpec((1, 1, d), lambda i: (row(i), 0, 1)),
                  _resident(wq.shape), _resident(wk.shape), _resident(wv.shape),
                  pl.BlockSpec((HEAD_DIM, 1), lambda i: (0, 0)),
                  pl.BlockSpec((1, LANES), lambda i: (0, 0))],
        out_specs=(pl.BlockSpec((groups, nblk, tm // TQ, LANES, TQ), lambda i: (0, 0, i, 0, 0)),
                   pl.BlockSpec((groups, tm, d), lambda i: (0, i, 0)),
                   pl.BlockSpec((groups, nblk, tm // TQ, LANES, TQ), lambda i: (0, 0, i, 0, 0))),
        compiler_params=_params("arbitrary"),
        name=f"qkv_proj_g{groups}",
    )(x, gain.reshape(1, d), mod, mod, wq, wk, wv, qg, kg)


def _online(idx, s, vt, m, l, acc_ref):
    m_new = jnp.maximum(m, jnp.max(s, axis=0, keepdims=True))
    alpha = jnp.exp(m - m_new)
    p = jnp.exp(s - m_new)
    l_new = alpha * l + jnp.sum(p, axis=0, keepdims=True)
    acc_ref[idx] = alpha * acc_ref[idx] + jnp.dot(vt, p.astype(BF16), preferred_element_type=F32)
    return m_new, l_new


def _attn_kernel(*refs, nds, diff, lambda_init):
    nseg = len(nds)
    segs = [refs[4 * s:4 * s + 4] for s in range(nseg)]
    pos = 4 * nseg
    if diff:
        lamv_ref, subg_ref = refs[pos], refs[pos + 1]
        pos += 2
    o_ref, acc_ref = refs[pos], refs[pos + 1]
    nq = segs[0][0].shape[2]
    top = lax.broadcasted_iota(jnp.int32, (LANES, TQ), 0) < HEAD_DIM
    if diff:
        lv = lamv_ref[...]
        lam = (jnp.exp(jnp.sum(lv[0:1] * lv[1:2], axis=-1, keepdims=True))
               - jnp.exp(jnp.sum(lv[2:3] * lv[3:4], axis=-1, keepdims=True)) + lambda_init)

    def q_tile(i, carry):
        acc_ref[...] = jnp.zeros_like(acc_ref)
        m0 = jnp.full((1, TQ), NEG_INF, F32)
        l0 = jnp.zeros((1, TQ), F32)
        st = (m0, l0, m0, l0)
        for (qt_ref, k_ref, vt_ref, b_ref), nd in zip(segs, nds):
            qt = qt_ref[0, 0, i]
            zero = jnp.zeros_like(qt)
            qa = jnp.where(top, qt, zero)
            qb = jnp.where(top, zero, qt)

            def k_tile(kj, st, k_ref=k_ref, vt_ref=vt_ref, b_ref=b_ref, qa=qa, qb=qb):
                ma, la, mb, lb = st
                koff = pl.multiple_of(kj * TQ, TQ)
                kk = k_ref[0, pl.ds(koff, TQ), :]
                vt = vt_ref[0, 0, kj]
                delta = i - kj
                sa = jnp.dot(kk, qa, preferred_element_type=F32) + b_ref[delta, 0]
                sb = jnp.dot(kk, qb, preferred_element_type=F32) + b_ref[delta, 1]
                ma, la = _online(0, sa, vt, ma, la, acc_ref)
                mb, lb = _online(1, sb, vt, mb, lb, acc_ref)
                return ma, la, mb, lb

            st = lax.fori_loop(jnp.maximum(i - (nd - 1), 0), i + 1, k_tile, st)
        ma, la, mb, lb = st
        oa = acc_ref[0] * (1.0 / la)
        ob = acc_ref[1] * (1.0 / lb)
        if diff:
            ot = oa - lam * ob
            ms = jnp.mean(ot * ot, axis=0, keepdims=True)
            ot = ((ot * lax.rsqrt(ms + RMS_EPS)) * subg_ref[...]) * (1.0 - lambda_init)
        else:
            ot = jnp.where(top, oa, ob)
        o_ref[pl.ds(pl.multiple_of(i * TQ, TQ), TQ), :] = ot.T.astype(BF16)
        return carry

    lax.fori_loop(0, nq, q_tile, 0)


def _attention(qt, k, vt, biases, *, seq, diff, lambda_init=0.0, lamv=None, subg=None):
    groups, nblk, ntile, _, _ = qt.shape
    t, d = k.shape[1], k.shape[2]
    bsz = t // seq
    nq = seq // TQ
    nds = tuple(b.shape[0] for b in biases)
    in_specs, args = [], []
    for g in range(groups):
        in_specs += [
            pl.BlockSpec((1, 1, nq, LANES, TQ), lambda j, b, g=g: (g, j, b, 0, 0)),
            pl.BlockSpec((1, seq, LANES), lambda j, b, g=g: (g, b, j)),
            pl.BlockSpec((1, 1, nq, LANES, TQ), lambda j, b, g=g: (g, j, b, 0, 0)),
            pl.BlockSpec((nds[g], 2, TQ, TQ), lambda j, b: (0, j, 0, 0)),
        ]
        args += [qt, k, vt, biases[g]]
    if diff:
        in_specs += [pl.BlockSpec(lamv.shape, lambda j, b: (0, 0)),
                     pl.BlockSpec((LANES, 1), lambda j, b: (0, 0))]
        args += [lamv, subg]
    return pl.pallas_call(
        functools.partial(_attn_kernel, nds=nds, diff=diff, lambda_init=lambda_init),
        out_shape=jax.ShapeDtypeStruct((t, d), BF16),
        grid=(nblk, bsz),
        in_specs=in_specs,
        out_specs=pl.BlockSpec((seq, LANES), lambda j, b: (b, j)),
        scratch_shapes=[pltpu.VMEM((2, LANES, TQ), F32)],
        compiler_params=_params("arbitrary", "arbitrary"),
        name="diff_attention" if diff else "dilated_attention",
    )(*args)


def _lru_kernel(x_ref, g_ref, sh_ref, sc_ref, win_ref, cw_ref, cb_ref, wx_ref, bx_ref,
                wa_ref, ba_ref, ap_ref, o_ref, uext_ref, h_ref):
    ts, d = x_ref.shape

    @pl.when(pl.program_id(1) == 0)
    def _():
        uext_ref[0:SUBLANES, :] = jnp.zeros((SUBLANES, d), F32)
        h_ref[...] = jnp.zeros_like(h_ref)

    h = _norm_mod(x_ref[...], g_ref[...], sh_ref[0], sc_ref[0]).astype(BF16)
    yu = jnp.dot(h, win_ref[...], preferred_element_type=F32)
    y = yu[:, :d]
    u = yu[:, d:]
    uext_ref[SUBLANES:SUBLANES + ts, :] = u
    conv = cb_ref[...] + u * cw_ref[0:1, :]
    for tap in range(1, CONV_WIDTH):
        conv = conv + uext_ref[SUBLANES - tap:SUBLANES - tap + ts, :] * cw_ref[tap:tap + 1, :]
    uext_ref[0:SUBLANES, :] = u[ts - SUBLANES:ts, :]
    uc = conv
    ucb = uc.astype(BF16)
    nblk = wx_ref.shape[0]
    bw = d // nblk
    gx = jnp.concatenate([jnp.dot(ucb[:, n * bw:(n + 1) * bw], wx_ref[n], preferred_element_type=F32)
                          for n in range(nblk)], axis=1) + bx_ref[...]
    ga = jnp.concatenate([jnp.dot(ucb[:, n * bw:(n + 1) * bw], wa_ref[n], preferred_element_type=F32)
                          for n in range(nblk)], axis=1) + ba_ref[...]
    gate_x = jax.nn.sigmoid(gx)
    gate_a = jax.nn.sigmoid(ga)
    z = -ap_ref[...]
    softplus = jnp.maximum(z, 0.0) + jnp.log1p(jnp.exp(-jnp.abs(z)))
    log_a = (-LRU_C * gate_a) * softplus
    a = jnp.exp(log_a)
    b = jnp.sqrt(-jnp.tanh(log_a) * (1.0 + a * a)) * (gate_x * uc)
    rows = lax.broadcasted_iota(jnp.int32, (ts, d), 0)
    step = 1
    while step < ts:
        keep = rows >= step
        a_prev = pltpu.roll(a, step, 0)
        b_prev = pltpu.roll(b, step, 0)
        b = jnp.where(keep, a * b_prev, 0.0) + b
        a = jnp.where(keep, a * a_prev, a)
        step *= 2
    hs = a * h_ref[...] + b
    h_ref[...] = hs[ts - 1:ts, :]
    gelu = 0.5 * y * (1.0 + jnp.tanh(math.sqrt(2.0 / math.pi) * (y + 0.044715 * (y * y * y))))
    o_ref[...] = (hs * gelu).astype(BF16)


def _lru(x, mod, layer, gain, w_in, conv_w, conv_b, w_x, b_x, w_a, b_a, a_param, *, seq, ts):
    t, d = x.shape
    bsz = t // seq
    ns = seq // ts
    vec = lambda v: v.reshape(1, d)
    return pl.pallas_call(
        _lru_kernel,
        out_shape=jax.ShapeDtypeStruct((t, d), BF16),
        grid=(bsz, ns),
        in_specs=[pl.BlockSpec((ts, d), lambda b, s: (b * ns + s, 0)),
                  pl.BlockSpec((1, d), lambda b, s: (0, 0)),
                  pl.BlockSpec((1, 1, d), lambda b, s: (layer * bsz + b, 0, 0)),
                  pl.BlockSpec((1, 1, d), lambda b, s: (layer * bsz + b, 0, 1)),
                  _resident(w_in.shape),
                  pl.BlockSpec((CONV_WIDTH, d), lambda b, s: (0, 0)),
                  pl.BlockSpec((1, d), lambda b, s: (0, 0)),
                  _resident(w_x.shape),
                  pl.BlockSpec((1, d), lambda b, s: (0, 0)),
                  _resident(w_a.shape),
                  pl.BlockSpec((1, d), lambda b, s: (0, 0)),
                  pl.BlockSpec((1, d), lambda b, s: (0, 0))],
        out_specs=pl.BlockSpec((ts, d), lambda b, s: (b * ns + s, 0)),
        scratch_shapes=[pltpu.VMEM((ts + SUBLANES, d), F32), pltpu.VMEM((1, d), F32)],
        compiler_params=_params("arbitrary", "arbitrary"),
        name="rglru_mixer",
    )(x, vec(gain), mod, mod, w_in, conv_w, vec(conv_b), w_x, vec(b_x), w_a, vec(b_a), vec(a_param))


def _post_mlp_kernel(x_ref, y_ref, wo_ref, g1_ref, g_ref, sh_ref, sc_ref, g2_ref, w1_ref, w2_ref,
                     o_ref, *, ff_chunk):
    x1 = x_ref[...] + g1_ref[0] * jnp.dot(y_ref[...], wo_ref[...], preferred_element_type=F32)
    h = _norm_mod(x1, g_ref[...], sh_ref[0], sc_ref[0]).astype(BF16)
    dff = w1_ref.shape[1]
    acc = None
    for c in range(dff // ff_chunk):
        a = jnp.dot(h, w1_ref[:, c * ff_chunk:(c + 1) * ff_chunk], preferred_element_type=F32)
        a = jnp.maximum(a, 0.0)
        a = (a * a).astype(BF16)
        part = jnp.dot(a, w2_ref[c * ff_chunk:(c + 1) * ff_chunk, :], preferred_element_type=F32)
        acc = part if acc is None else acc + part
    o_ref[...] = x1 + g2_ref[0] * acc


def _post_mlp(x, y, mod, layer, w_o, gain, w1, w2, *, seq, tm, ff_chunk):
    t, d = x.shape
    bsz = t // seq
    row = lambda i: layer * bsz + (i * tm) // seq
    part = lambda p: pl.BlockSpec((1, 1, d), lambda i: (row(i), 0, p))
    return pl.pallas_call(
        functools.partial(_post_mlp_kernel, ff_chunk=ff_chunk),
        out_shape=jax.ShapeDtypeStruct((t, d), F32),
        grid=(t // tm,),
        in_specs=[pl.BlockSpec((tm, d), lambda i: (i, 0)),
                  pl.BlockSpec((tm, d), lambda i: (i, 0)),
                  _resident(w_o.shape),
                  part(2),
                  pl.BlockSpec((1, d), lambda i: (0, 0)),
                  part(3), part(4), part(5),
                  _resident(w1.shape), _resident(w2.shape)],
        out_specs=pl.BlockSpec((tm, d), lambda i: (i, 0)),
        compiler_params=_params("arbitrary"),
        name="outproj_mlp",
    )(x, y, w_o, mod, gain.reshape(1, d), mod, mod, mod, w1, w2)


def kernel(x, c, rel_bias, w_ada, b_ada, norm_mix, norm_mlp, mlp_w1, mlp_w2, da_w_qkv, da_w_o, da_q_gain, da_k_gain, da_lam_q1, da_lam_k1, da_lam_q2, da_lam_k2, da_sub_gain, lru_w_in, lru_conv_w, lru_conv_b, lru_w_x, lru_b_x, lru_w_a, lru_b_a, lru_a_param, lru_w_out, dil_w_qkv, dil_w_o, dil_q_gain, dil_k_gain):
    bsz, seq, d = x.shape
    depth = w_ada.shape[0]
    t = bsz * seq
    nq = seq // TQ
    assert seq % TQ == 0 and d % LANES == 0

    mod = _ada(c, w_ada, b_ada).reshape(depth * bsz, 1, 6 * d)

    ncols = rel_bias.shape[1]
    rb_ext = jnp.concatenate(
        [rel_bias.T.astype(F32), jnp.full((ncols, 1), NEG_INF, F32),
         jnp.zeros((ncols, BUCKET_PAD - NUM_BUCKETS - 1), F32)], axis=1)
    kinds = [i % N_MIXERS for i in range(depth)]
    da_bias = _bias_tiles(rb_ext, nq, None, None) if 0 in kinds else None
    dil_bias = None
    if 2 in kinds:
        dil_bias = [_bias_tiles(rb_ext, min((w + TQ - 1) // TQ, nq - 1) + 1, r, w)
                    for (w, r) in DIL_PATTERNS]

    q_scale = HEAD_DIM ** -0.5
    xf = x.reshape(t, d)
    for i in range(depth):
        kind, slot = kinds[i], i // N_MIXERS
        if kind == 0:
            w = da_w_qkv[slot]
            wq = w[:, :d].T.astype(BF16)[None]
            wk = w[:, d:2 * d].astype(BF16)[None]
            wv = w[:, 2 * d:].T.astype(BF16)[None]
            qg = (da_q_gain[slot] * q_scale).reshape(HEAD_DIM, 1)
            kg = jnp.tile(da_k_gain[slot], 2).reshape(1, LANES)
            qt, k, vt = _qkv(xf, mod, i, norm_mix[i], wq, wk, wv, qg, kg, seq=seq, tm=2 * TQ)
            lambda_init = 0.8 - 0.6 * math.exp(-0.3 * i)
            lamv = jnp.stack([da_lam_q1[slot], da_lam_k1[slot], da_lam_q2[slot], da_lam_k2[slot]]).astype(F32)
            y = _attention(qt, k, vt, [da_bias], seq=seq, diff=True, lambda_init=lambda_init,
                           lamv=lamv, subg=da_sub_gain[slot].reshape(LANES, 1))
            w_o = da_w_o[slot]
        elif kind == 1:
            y = _lru(xf, mod, i, norm_mix[i], lru_w_in[slot].astype(BF16), lru_conv_w[slot],
                     lru_conv_b[slot], lru_w_x[slot].astype(BF16), lru_b_x[slot],
                     lru_w_a[slot].astype(BF16), lru_b_a[slot], lru_a_param[slot], seq=seq, ts=TQ)
            w_o = lru_w_out[slot]
        else:
            w = dil_w_qkv[slot].reshape(d, len(DIL_PATTERNS), 3, d)
            wq = jnp.transpose(w[:, :, 0, :], (1, 2, 0)).astype(BF16)
            wk = jnp.transpose(w[:, :, 1, :], (1, 0, 2)).astype(BF16)
            wv = jnp.transpose(w[:, :, 2, :], (1, 2, 0)).astype(BF16)
            qg = (dil_q_gain[slot] * q_scale).reshape(HEAD_DIM, 1)
            kg = jnp.tile(dil_k_gain[slot], 2).reshape(1, LANES)
            qt, k, vt = _qkv(xf, mod, i, norm_mix[i], wq, wk, wv, qg, kg, seq=seq, tm=TQ)
            y = _attention(qt, k, vt, dil_bias, seq=seq, diff=False)
            w_o = dil_w_o[slot]
        xf = _post_mlp(xf, y, mod, i, w_o.astype(BF16), norm_mlp[i], mlp_w1[i].astype(BF16),
                       mlp_w2[i].astype(BF16), seq=seq, tm=2 * TQ, ff_chunk=1024)
    return xf.reshape(bsz, seq, d)
```

```python
import functools
import math

import numpy as np
import jax
import jax.numpy as jnp
from jax import lax
from jax.experimental import pallas as pl
from jax.experimental.pallas import tpu as pltpu

F32 = jnp.float32
BF16 = jnp.bfloat16

RMS_EPS = 1e-6
NEG_INF = -1e30
NUM_BUCKETS = 32
REL_MAX_DISTANCE = 2048
HEAD_DIM = 64
N_MIXERS = 3
CONV_WIDTH = 4
LRU_C = 8.0
DIL_PATTERNS = ((128, 1), (512, 4), (2048, 16))
LOG2E = math.log2(math.e)
MASKED_BUCKET = NUM_BUCKETS
BUCKET_PAD = 64

LANES = 128
SUBLANES = 8
MXU_WIDTH = 256
VMEM_LIMIT_BYTES = 56 * 1024 * 1024

TQ = MXU_WIDTH


def _params(*sem):
    return pltpu.CompilerParams(dimension_semantics=sem, vmem_limit_bytes=VMEM_LIMIT_BYTES)


def _resident(shape):
    nd = len(shape)
    return pl.BlockSpec(shape, lambda *_: (0,) * nd, pipeline_mode=pl.Buffered(1))


def _norm_mod(x, gain, shift, scale):
    ms = jnp.mean(x * x, axis=-1, keepdims=True)
    y = (x * lax.rsqrt(ms + RMS_EPS)) * gain
    return y * (1.0 + scale) + shift


def _ada_kernel(c_ref, w_ref, b_ref, o_ref):
    c = c_ref[...]
    a = (c * jax.nn.sigmoid(c)).astype(BF16)
    o_ref[0] = jnp.dot(a, w_ref[0].astype(BF16), preferred_element_type=F32) + b_ref[0]


def _ada(c, w_ada, b_ada):
    depth, d, n = w_ada.shape
    bsz = c.shape[0]
    tn = n // 4
    return pl.pallas_call(
        _ada_kernel,
        out_shape=jax.ShapeDtypeStruct((depth, bsz, n), F32),
        grid=(depth, n // tn),
        in_specs=[pl.BlockSpec((bsz, d), lambda l, j: (0, 0)),
                  pl.BlockSpec((1, d, tn), lambda l, j: (l, 0, j)),
                  pl.BlockSpec((1, 1, tn), lambda l, j: (l, 0, j))],
        out_specs=pl.BlockSpec((1, bsz, tn), lambda l, j: (l, 0, j)),
        compiler_params=_params("arbitrary", "arbitrary"),
        name="ada_mod",
    )(c, w_ada, b_ada.reshape(depth, 1, n))


def _t5_bucket_np(n):
    n = np.maximum(n, 0)
    max_exact = NUM_BUCKETS // 2
    nf = np.maximum(n, max_exact).astype(np.float32)
    large = max_exact + (np.log(nf / np.float32(max_exact)) / np.float32(math.log(REL_MAX_DISTANCE / max_exact))
                         * np.float32(NUM_BUCKETS - max_exact)).astype(np.int32)
    large = np.minimum(large, NUM_BUCKETS - 1)
    return np.where(n < max_exact, n, large)


def _bucket_tiles(nd, dilation, window):
    key = np.arange(TQ)[:, None]
    qry = np.arange(TQ)[None, :]
    tiles = []
    for delta in range(nd):
        dist = TQ * delta + qry - key
        valid = dist >= 0
        if dilation is not None:
            valid &= (dist % dilation == 0) & (dist <= window)
        bucket = _t5_bucket_np(dist)
        tiles.append(np.where(valid, bucket, MASKED_BUCKET).reshape(1, TQ * TQ))
    return np.stack(tiles).astype(np.int32)


def _bias_kernel(bk_ref, rb_ref, o_ref):
    rb = rb_ref[...] * LOG2E
    hi = rb.astype(BF16)
    r1 = rb - hi.astype(F32)
    mid = r1.astype(BF16)
    low = (r1 - mid.astype(F32)).astype(BF16)
    n = o_ref.shape[2]
    chunk = 8192
    for c in range(n // chunk):
        bk = bk_ref[0, :, c * chunk:(c + 1) * chunk]
        onehot = lax.broadcasted_iota(jnp.int32, (BUCKET_PAD, chunk), 0) == bk
        oh = jnp.where(onehot, 1.0, 0.0).astype(BF16)
        o_ref[0, :, c * chunk:(c + 1) * chunk] = (
            jnp.dot(hi, oh, preferred_element_type=F32)
            + jnp.dot(mid, oh, preferred_element_type=F32)
            + jnp.dot(low, oh, preferred_element_type=F32))


def _bias_tiles(rb_ext, nd, dilation, window):
    buckets = jnp.asarray(_bucket_tiles(nd, dilation, window))
    ncols = rb_ext.shape[0]
    out = pl.pallas_call(
        _bias_kernel,
        out_shape=jax.ShapeDtypeStruct((nd, ncols, TQ * TQ), F32),
        grid=(nd,),
        in_specs=[pl.BlockSpec((1, 1, TQ * TQ), lambda i: (i, 0, 0)),
                  pl.BlockSpec((ncols, BUCKET_PAD), lambda i: (0, 0))],
        out_specs=pl.BlockSpec((1, ncols, TQ * TQ), lambda i: (i, 0, 0)),
        compiler_params=_params("arbitrary"),
        name="rel_bias_tiles",
    )(buckets, rb_ext)
    return out.reshape(nd, ncols, TQ, TQ)


def _qkv_kernel(x_ref, g_ref, sh_ref, sc_ref, wq_ref, wk_ref, wv_ref, qg_ref, kg_ref,
                qt_ref, k_ref, vt_ref):
    x = x_ref[...]
    tm, d = x.shape
    groups = wq_ref.shape[0]
    h = _norm_mod(x, g_ref[...], sh_ref[0], sc_ref[0]).astype(BF16)
    nt = (((1,), (1,)), ((), ()))
    lo = lax.broadcasted_iota(jnp.int32, (tm, LANES), 1) < HEAD_DIM
    nsub = tm // TQ
    for g in range(groups):
        yq = lax.dot_general(wq_ref[g], h, nt, preferred_element_type=F32)
        y3 = yq.reshape(d // HEAD_DIM, HEAD_DIM, tm)
        ms = jnp.mean(y3 * y3, axis=1, keepdims=True)
        y3 = (y3 * lax.rsqrt(ms + RMS_EPS)) * qg_ref[...][None]
        yb = y3.reshape(d // LANES, LANES, tm).astype(BF16)
        for t in range(nsub):
            qt_ref[g, :, t] = yb[:, :, t * TQ:(t + 1) * TQ]
        yk = jnp.dot(h, wk_ref[g], preferred_element_type=F32)
        for c in range(d // LANES):
            yc = yk[:, c * LANES:(c + 1) * LANES]
            y2 = yc * yc
            s_lo = jnp.sum(jnp.where(lo, y2, 0.0), axis=-1, keepdims=True)
            s_hi = jnp.sum(jnp.where(lo, 0.0, y2), axis=-1, keepdims=True)
            ms = jnp.where(lo, s_lo, s_hi) * (1.0 / HEAD_DIM)
            k_ref[g, :, c * LANES:(c + 1) * LANES] = (
                (yc * lax.rsqrt(ms + RMS_EPS)) * kg_ref[...]).astype(BF16)
        yv = lax.dot_general(wv_ref[g], h, nt, preferred_element_type=F32)
        yvb = yv.reshape(d // LANES, LANES, tm).astype(BF16)
        for t in range(nsub):
            vt_ref[g, :, t] = yvb[:, :, t * TQ:(t + 1) * TQ]


def _qkv(x, mod, layer, gain, wq, wk, wv, qg, kg, *, seq, tm):
    t, d = x.shape
    groups = wq.shape[0]
    bsz = t // seq
    nblk = d // LANES
    row = lambda i: layer * bsz + (i * tm) // seq
    tshape = (groups, nblk, t // TQ, LANES, TQ)
    return pl.pallas_call(
        _qkv_kernel,
        out_shape=(jax.ShapeDtypeStruct(tshape, BF16),
                   jax.ShapeDtypeStruct((groups, t, d), BF16),
                   jax.ShapeDtypeStruct(tshape, BF16)),
        grid=(t // tm,),
        in_specs=[pl.BlockSpec((tm, d), lambda i: (i, 0)),
                  pl.BlockSpec((1, d), lambda i: (0, 0)),
                  pl.BlockSpec((1, 1, d), lambda i: (row(i), 0, 0)),
                  pl.BlockSpec((1, 1, d), lambda i: (row(i), 0, 1)),
                  _resident(wq.shape), _resident(wk.shape), _resident(wv.shape),
                  pl.BlockSpec((HEAD_DIM, 1), lambda i: (0, 0)),
                  pl.BlockSpec((1, LANES), lambda i: (0, 0))],
        out_specs=(pl.BlockSpec((groups, nblk, tm // TQ, LANES, TQ), lambda i: (0, 0, i, 0, 0)),
                   pl.BlockSpec((groups, tm, d), lambda i: (0, i, 0)),
                   pl.BlockSpec((groups, nblk, tm // TQ, LANES, TQ), lambda i: (0, 0, i, 0, 0))),
        compiler_params=_params("arbitrary"),
        name=f"qkv_proj_g{groups}",
    )(x, gain.reshape(1, d), mod, mod, wq, wk, wv, qg, kg)


def _attn_kernel(*refs, nds, diff, lambda_init):
    nseg = len(nds)
    segs = [refs[4 * s:4 * s + 4] for s in range(nseg)]
    pos = 4 * nseg
    if diff:
        lamv_ref, subg_ref = refs[pos], refs[pos + 1]
        pos += 2
    o_ref, s_ref = refs[pos], refs[pos + 1]
    nq = segs[0][0].shape[2]
    top = lax.broadcasted_iota(jnp.int32, (LANES, TQ), 0) < HEAD_DIM
    if diff:
        lv = lamv_ref[...]
        lam = (jnp.exp(jnp.sum(lv[0:1] * lv[1:2], axis=-1, keepdims=True))
               - jnp.exp(jnp.sum(lv[2:3] * lv[3:4], axis=-1, keepdims=True)) + lambda_init)

    for i in range(nq):
        par = i % 2
        tiles = [(s, kj) for s, nd in enumerate(nds) for kj in range(max(i - nd + 1, 0), i + 1)]
        qsel = []
        for qt_ref, _, _, _ in segs:
            qt = qt_ref[0, 0, i]
            zero = jnp.zeros_like(qt)
            qsel.append((jnp.where(top, qt, zero), jnp.where(top, zero, qt)))
        m = [None, None]
        for n, (s, kj) in enumerate(tiles):
            _, k_ref, _, b_ref = segs[s]
            kk = k_ref[0, kj * TQ:(kj + 1) * TQ, :]
            for st in range(2):
                sc = jnp.dot(kk, qsel[s][st], preferred_element_type=F32) + b_ref[i - kj, st]
                s_ref[par, st, n] = sc
                cm = jnp.max(sc, axis=0, keepdims=True)
                m[st] = cm if m[st] is None else jnp.maximum(m[st], cm)
        l = [None, None]
        acc = [None, None]
        for n, (s, kj) in enumerate(tiles):
            vt = segs[s][2][0, 0, kj]
            for st in range(2):
                p = jnp.exp2(s_ref[par, st, n] - m[st])
                ps = jnp.sum(p, axis=0, keepdims=True)
                pv = jnp.dot(vt, p.astype(BF16), preferred_element_type=F32)
                l[st] = ps if l[st] is None else l[st] + ps
                acc[st] = pv if acc[st] is None else acc[st] + pv
        oa = acc[0] * (1.0 / l[0])
        ob = acc[1] * (1.0 / l[1])
        if diff:
            ot = oa - lam * ob
            ms = jnp.mean(ot * ot, axis=0, keepdims=True)
            ot = ((ot * lax.rsqrt(ms + RMS_EPS)) * subg_ref[...]) * (1.0 - lambda_init)
        else:
            ot = jnp.where(top, oa, ob)
        o_ref[i * TQ:(i + 1) * TQ, :] = ot.T.astype(BF16)


def _attention(qt, k, vt, biases, *, seq, diff, lambda_init=0.0, lamv=None, subg=None):
    groups, nblk, ntile, _, _ = qt.shape
    t, d = k.shape[1], k.shape[2]
    bsz = t // seq
    nq = seq // TQ
    nds = tuple(b.shape[0] for b in biases)
    in_specs, args = [], []
    for g in range(groups):
        in_specs += [
            pl.BlockSpec((1, 1, nq, LANES, TQ), lambda j, b, g=g: (g, j, b, 0, 0)),
            pl.BlockSpec((1, seq, LANES), lambda j, b, g=g: (g, b, j)),
            pl.BlockSpec((1, 1, nq, LANES, TQ), lambda j, b, g=g: (g, j, b, 0, 0)),
            pl.BlockSpec((nds[g], 2, TQ, TQ), lambda j, b: (0, j, 0, 0)),
        ]
        args += [qt, k, vt, biases[g]]
    if diff:
        in_specs += [pl.BlockSpec(lamv.shape, lambda j, b: (0, 0)),
                     pl.BlockSpec((LANES, 1), lambda j, b: (0, 0))]
        args += [lamv, subg]
    return pl.pallas_call(
        functools.partial(_attn_kernel, nds=nds, diff=diff, lambda_init=lambda_init),
        out_shape=jax.ShapeDtypeStruct((t, d), BF16),
        grid=(nblk, bsz),
        in_specs=in_specs,
        out_specs=pl.BlockSpec((seq, LANES), lambda j, b: (b, j)),
        scratch_shapes=[pltpu.VMEM((2, 2, sum(min(nd, nq) for nd in nds), TQ, TQ), F32)],
        compiler_params=_params("arbitrary", "arbitrary"),
        name="diff_attention" if diff else "dilated_attention",
    )(*args)


def _lru_kernel(x_ref, g_ref, sh_ref, sc_ref, win_ref, cw_ref, cb_ref, wx_ref, bx_ref,
                wa_ref, ba_ref, ap_ref, o_ref, uext_ref, h_ref):
    ts, d = x_ref.shape

    @pl.when(pl.program_id(1) == 0)
    def _():
        uext_ref[0:SUBLANES, :] = jnp.zeros((SUBLANES, d), F32)
        h_ref[...] = jnp.zeros_like(h_ref)

    h = _norm_mod(x_ref[...], g_ref[...], sh_ref[0], sc_ref[0]).astype(BF16)
    yu = jnp.dot(h, win_ref[...], preferred_element_type=F32)
    y = yu[:, :d]
    u = yu[:, d:]
    uext_ref[SUBLANES:SUBLANES + ts, :] = u
    conv = cb_ref[...] + u * cw_ref[0:1, :]
    for tap in range(1, CONV_WIDTH):
        conv = conv + uext_ref[SUBLANES - tap:SUBLANES - tap + ts, :] * cw_ref[tap:tap + 1, :]
    uext_ref[0:SUBLANES, :] = u[ts - SUBLANES:ts, :]
    uc = conv
    ucb = uc.astype(BF16)
    nblk = wx_ref.shape[0]
    bw = d // nblk
    gx = jnp.concatenate([jnp.dot(ucb[:, n * bw:(n + 1) * bw], wx_ref[n], preferred_element_type=F32)
                          for n in range(nblk)], axis=1) + bx_ref[...]
    ga = jnp.concatenate([jnp.dot(ucb[:, n * bw:(n + 1) * bw], wa_ref[n], preferred_element_type=F32)
                          for n in range(nblk)], axis=1) + ba_ref[...]
    gate_x = jax.nn.sigmoid(gx)
    gate_a = jax.nn.sigmoid(ga)
    z = -ap_ref[...]
    softplus = jnp.maximum(z, 0.0) + jnp.log1p(jnp.exp(-jnp.abs(z)))
    log_a = (-LRU_C * gate_a) * softplus
    a = jnp.exp(log_a)
    b = jnp.sqrt(-jnp.tanh(log_a) * (1.0 + a * a)) * (gate_x * uc)
    rows = lax.broadcasted_iota(jnp.int32, (ts, d), 0)
    step = 1
    while step < ts:
        keep = rows >= step
        a_prev = pltpu.roll(a, step, 0)
        b_prev = pltpu.roll(b, step, 0)
        b = jnp.where(keep, a * b_prev, 0.0) + b
        a = jnp.where(keep, a * a_prev, a)
        step *= 2
    hs = a * h_ref[...] + b
    h_ref[...] = hs[ts - 1:ts, :]
    gelu = 0.5 * y * (1.0 + jnp.tanh(math.sqrt(2.0 / math.pi) * (y + 0.044715 * (y * y * y))))
    o_ref[...] = (hs * gelu).astype(BF16)


def _lru(x, mod, layer, gain, w_in, conv_w, conv_b, w_x, b_x, w_a, b_a, a_param, *, seq, ts):
    t, d = x.shape
    bsz = t // seq
    ns = seq // ts
    vec = lambda v: v.reshape(1, d)
    return pl.pallas_call(
        _lru_kernel,
        out_shape=jax.ShapeDtypeStruct((t, d), BF16),
        grid=(bsz, ns),
        in_specs=[pl.BlockSpec((ts, d), lambda b, s: (b * ns + s, 0)),
                  pl.BlockSpec((1, d), lambda b, s: (0, 0)),
                  pl.BlockSpec((1, 1, d), lambda b, s: (layer * bsz + b, 0, 0)),
                  pl.BlockSpec((1, 1, d), lambda b, s: (layer * bsz + b, 0, 1)),
                  _resident(w_in.shape),
                  pl.BlockSpec((CONV_WIDTH, d), lambda b, s: (0, 0)),
                  pl.BlockSpec((1, d), lambda b, s: (0, 0)),
                  _resident(w_x.shape),
                  pl.BlockSpec((1, d), lambda b, s: (0, 0)),
                  _resident(w_a.shape),
                  pl.BlockSpec((1, d), lambda b, s: (0, 0)),
                  pl.BlockSpec((1, d), lambda b, s: (0, 0))],
        out_specs=pl.BlockSpec((ts, d), lambda b, s: (b * ns + s, 0)),
        scratch_shapes=[pltpu.VMEM((ts + SUBLANES, d), F32), pltpu.VMEM((1, d), F32)],
        compiler_params=_params("arbitrary", "arbitrary"),
        name="rglru_mixer",
    )(x, vec(gain), mod, mod, w_in, conv_w, vec(conv_b), w_x, vec(b_x), w_a, vec(b_a), vec(a_param))


def _post_mlp_kernel(x_ref, y_ref, wo_ref, g1_ref, g_ref, sh_ref, sc_ref, g2_ref, w1_ref, w2_ref,
                     o_ref, *, ff_chunk):
    x1 = x_ref[...] + g1_ref[0] * jnp.dot(y_ref[...], wo_ref[...], preferred_element_type=F32)
    h = _norm_mod(x1, g_ref[...], sh_ref[0], sc_ref[0]).astype(BF16)
    dff = w1_ref.shape[1]
    acc = None
    for c in range(dff // ff_chunk):
        a = jnp.dot(h, w1_ref[:, c * ff_chunk:(c + 1) * ff_chunk], preferred_element_type=F32)
        a = jnp.maximum(a, 0.0)
        a = (a * a).astype(BF16)
        part = jnp.dot(a, w2_ref[c * ff_chunk:(c + 1) * ff_chunk, :], preferred_element_type=F32)
        acc = part if acc is None else acc + part
    o_ref[...] = x1 + g2_ref[0] * acc


def _post_mlp(x, y, mod, layer, w_o, gain, w1, w2, *, seq, tm, ff_chunk):
    t, d = x.shape
    bsz = t // seq
    row = lambda i: layer * bsz + (i * tm) // seq
    part = lambda p: pl.BlockSpec((1, 1, d), lambda i: (row(i), 0, p))
    return pl.pallas_call(
        functools.partial(_post_mlp_kernel, ff_chunk=ff_chunk),
        out_shape=jax.ShapeDtypeStruct((t, d), F32),
        grid=(t // tm,),
        in_specs=[pl.BlockSpec((tm, d), lambda i: (i, 0)),
                  pl.BlockSpec((tm, d), lambda i: (i, 0)),
                  _resident(w_o.shape),
                  part(2),
                  pl.BlockSpec((1, d), lambda i: (0, 0)),
                  part(3), part(4), part(5),
                  _resident(w1.shape), _resident(w2.shape)],
        out_specs=pl.BlockSpec((tm, d), lambda i: (i, 0)),
        compiler_params=_params("arbitrary"),
        name="outproj_mlp",
    )(x, y, w_o, mod, gain.reshape(1, d), mod, mod, mod, w1, w2)


def kernel(x, c, rel_bias, w_ada, b_ada, norm_mix, norm_mlp, mlp_w1, mlp_w2, da_w_qkv, da_w_o, da_q_gain, da_k_gain, da_lam_q1, da_lam_k1, da_lam_q2, da_lam_k2, da_sub_gain, lru_w_in, lru_conv_w, lru_conv_b, lru_w_x, lru_b_x, lru_w_a, lru_b_a, lru_a_param, lru_w_out, dil_w_qkv, dil_w_o, dil_q_gain, dil_k_gain):
    bsz, seq, d = x.shape
    depth = w_ada.shape[0]
    t = bsz * seq
    nq = seq // TQ
    assert seq % TQ == 0 and d % LANES == 0

    mod = _ada(c, w_ada, b_ada).reshape(depth * bsz, 1, 6 * d)

    ncols = rel_bias.shape[1]
    rb_ext = jnp.concatenate(
        [rel_bias.T.astype(F32), jnp.full((ncols, 1), NEG_INF, F32),
         jnp.zeros((ncols, BUCKET_PAD - NUM_BUCKETS - 1), F32)], axis=1)
    kinds = [i % N_MIXERS for i in range(depth)]
    da_bias = _bias_tiles(rb_ext, nq, None, None) if 0 in kinds else None
    dil_bias = None
    if 2 in kinds:
        dil_bias = [_bias_tiles(rb_ext, min((w + TQ - 1) // TQ, nq - 1) + 1, r, w)
                    for (w, r) in DIL_PATTERNS]

    q_scale = HEAD_DIM ** -0.5 * LOG2E
    xf = x.reshape(t, d)
    for i in range(depth):
        kind, slot = kinds[i], i // N_MIXERS
        if kind == 0:
            w = da_w_qkv[slot]
            wq = w[:, :d].T.astype(BF16)[None]
            wk = w[:, d:2 * d].astype(BF16)[None]
            wv = w[:, 2 * d:].T.astype(BF16)[None]
            qg = (da_q_gain[slot] * q_scale).reshape(HEAD_DIM, 1)
            kg = jnp.tile(da_k_gain[slot], 2).reshape(1, LANES)
            qt, k, vt = _qkv(xf, mod, i, norm_mix[i], wq, wk, wv, qg, kg, seq=seq, tm=2 * TQ)
            lambda_init = 0.8 - 0.6 * math.exp(-0.3 * i)
            lamv = jnp.stack([da_lam_q1[slot], da_lam_k1[slot], da_lam_q2[slot], da_lam_k2[slot]]).astype(F32)
            y = _attention(qt, k, vt, [da_bias], seq=seq, diff=True, lambda_init=lambda_init,
                           lamv=lamv, subg=da_sub_gain[slot].reshape(LANES, 1))
            w_o = da_w_o[slot]
        elif kind == 1:
            y = _lru(xf, mod, i, norm_mix[i], lru_w_in[slot].astype(BF16), lru_conv_w[slot],
                     lru_conv_b[slot], lru_w_x[slot].astype(BF16), lru_b_x[slot],
                     lru_w_a[slot].astype(BF16), lru_b_a[slot], lru_a_param[slot], seq=seq, ts=TQ)
            w_o = lru_w_out[slot]
        else:
            w = dil_w_qkv[slot].reshape(d, len(DIL_PATTERNS), 3, d)
            wq = jnp.transpose(w[:, :, 0, :], (1, 2, 0)).astype(BF16)
            wk = jnp.transpose(w[:, :, 1, :], (1, 0, 2)).astype(BF16)
            wv = jnp.transpose(w[:, :, 2, :], (1, 2, 0)).astype(BF16)
            qg = (dil_q_gain[slot] * q_scale).reshape(HEAD_DIM, 1)
            kg = jnp.tile(dil_k_gain[slot], 2).reshape(1, LANES)
            qt, k, vt = _qkv(xf, mod, i, norm_mix[i], wq, wk, wv, qg, kg, seq=seq, tm=TQ)
            y = _attention(qt, k, vt, dil_bias, seq=seq, diff=False)
            w_o = dil_w_o[slot]
        xf = _post_mlp(xf, y, mod, i, w_o.astype(BF16), norm_mlp[i], mlp_w1[i].astype(BF16),
                       mlp_w2[i].astype(BF16), seq=seq, tm=2 * TQ, ff_chunk=1024)
    return xf.reshape(bsz, seq, d)
```

```python
import functools
import math

import numpy as np
import jax
import jax.numpy as jnp
from jax import lax
from jax.experimental import pallas as pl
from jax.experimental.pallas import tpu as pltpu

F32 = jnp.float32
BF16 = jnp.bfloat16

RMS_EPS = 1e-6
NEG_INF = -1e30
NUM_BUCKETS = 32
REL_MAX_DISTANCE = 2048
HEAD_DIM = 64
N_MIXERS = 3
CONV_WIDTH = 4
LRU_C = 8.0
DIL_PATTERNS = ((128, 1), (512, 4), (2048, 16))
LOG2E = math.log2(math.e)
MASKED_BUCKET = NUM_BUCKETS
BUCKET_PAD = 64

LANES = 128
SUBLANES = 8
BF16_ROWS = 16
MXU_WIDTH = 256
VMEM_LIMIT_BYTES = 56 * 1024 * 1024

TQ = MXU_WIDTH
PASS1_MARGIN = 2


def _params(*sem):
    return pltpu.CompilerParams(dimension_semantics=sem, vmem_limit_bytes=VMEM_LIMIT_BYTES)


def _resident(shape):
    nd = len(shape)
    return pl.BlockSpec(shape, lambda *_: (0,) * nd, pipeline_mode=pl.Buffered(1))


def _norm_mod(x, gain, shift, scale):
    ms = jnp.mean(x * x, axis=-1, keepdims=True)
    y = (x * lax.rsqrt(ms + RMS_EPS)) * gain
    return y * (1.0 + scale) + shift


def _ada_kernel(c_ref, w_ref, b_ref, o_ref):
    c = c_ref[...]
    a = (c * jax.nn.sigmoid(c)).astype(BF16)
    o_ref[0] = jnp.dot(a, w_ref[0].astype(BF16), preferred_element_type=F32) + b_ref[0]


def _ada(c, w_ada, b_ada):
    depth, d, n = w_ada.shape
    bsz = c.shape[0]
    tn = n // 4
    return pl.pallas_call(
        _ada_kernel,
        out_shape=jax.ShapeDtypeStruct((depth, bsz, n), F32),
        grid=(depth, n // tn),
        in_specs=[pl.BlockSpec((bsz, d), lambda l, j: (0, 0)),
                  pl.BlockSpec((1, d, tn), lambda l, j: (l, 0, j)),
                  pl.BlockSpec((1, 1, tn), lambda l, j: (l, 0, j))],
        out_specs=pl.BlockSpec((1, bsz, tn), lambda l, j: (l, 0, j)),
        compiler_params=_params("arbitrary", "arbitrary"),
        name="ada_mod",
    )(c, w_ada, b_ada.reshape(depth, 1, n))


def _t5_bucket_np(n):
    n = np.maximum(n, 0)
    max_exact = NUM_BUCKETS // 2
    nf = np.maximum(n, max_exact).astype(np.float32)
    large = max_exact + (np.log(nf / np.float32(max_exact)) / np.float32(math.log(REL_MAX_DISTANCE / max_exact))
                         * np.float32(NUM_BUCKETS - max_exact)).astype(np.int32)
    large = np.minimum(large, NUM_BUCKETS - 1)
    return np.where(n < max_exact, n, large)


def _bucket_tiles(nd, dilation, window):
    key = np.arange(TQ)[:, None]
    qry = np.arange(TQ)[None, :]
    tiles = []
    for delta in range(nd):
        dist = TQ * delta + qry - key
        valid = dist >= 0
        if dilation is not None:
            valid &= (dist % dilation == 0) & (dist <= window)
        bucket = _t5_bucket_np(dist)
        tiles.append(np.where(valid, bucket, MASKED_BUCKET).reshape(1, TQ * TQ))
    return np.stack(tiles).astype(np.int32)


def _bias_kernel(bk_ref, rb_ref, o_ref):
    rb = rb_ref[...] * LOG2E
    hi = rb.astype(BF16)
    r1 = rb - hi.astype(F32)
    mid = r1.astype(BF16)
    low = (r1 - mid.astype(F32)).astype(BF16)
    n = o_ref.shape[2]
    chunk = 8192
    for c in range(n // chunk):
        bk = bk_ref[0, :, c * chunk:(c + 1) * chunk]
        onehot = lax.broadcasted_iota(jnp.int32, (BUCKET_PAD, chunk), 0) == bk
        oh = jnp.where(onehot, 1.0, 0.0).astype(BF16)
        o_ref[0, :, c * chunk:(c + 1) * chunk] = (
            jnp.dot(hi, oh, preferred_element_type=F32)
            + jnp.dot(mid, oh, preferred_element_type=F32)
            + jnp.dot(low, oh, preferred_element_type=F32))


def _bias_tiles(rb_ext, nd, dilation, window):
    buckets = jnp.asarray(_bucket_tiles(nd, dilation, window))
    ncols = rb_ext.shape[0]
    out = pl.pallas_call(
        _bias_kernel,
        out_shape=jax.ShapeDtypeStruct((nd, ncols, TQ * TQ), F32),
        grid=(nd,),
        in_specs=[pl.BlockSpec((1, 1, TQ * TQ), lambda i: (i, 0, 0)),
                  pl.BlockSpec((ncols, BUCKET_PAD), lambda i: (0, 0))],
        out_specs=pl.BlockSpec((1, ncols, TQ * TQ), lambda i: (i, 0, 0)),
        compiler_params=_params("arbitrary"),
        name="rel_bias_tiles",
    )(buckets, rb_ext)
    return out.reshape(nd, ncols, TQ, TQ)


def _qkv_kernel(x_ref, g_ref, sh_ref, sc_ref, wq_ref, wk_ref, wv_ref, qg_ref, kg_ref,
                qt_ref, k_ref, vt_ref):
    x = x_ref[...]
    tm, d = x.shape
    groups = wq_ref.shape[0]
    h = _norm_mod(x, g_ref[...], sh_ref[0], sc_ref[0]).astype(BF16)
    nt = (((1,), (1,)), ((), ()))
    lo = lax.broadcasted_iota(jnp.int32, (tm, LANES), 1) < HEAD_DIM
    nsub = tm // TQ
    for g in range(groups):
        yq = lax.dot_general(wq_ref[g], h, nt, preferred_element_type=F32)
        y3 = yq.reshape(d // HEAD_DIM, HEAD_DIM, tm)
        ms = jnp.mean(y3 * y3, axis=1, keepdims=True)
        y3 = (y3 * lax.rsqrt(ms + RMS_EPS)) * qg_ref[...][None]
        yb = y3.reshape(d // LANES, LANES, tm).astype(BF16)
        for t in range(nsub):
            qt_ref[g, :, t] = yb[:, :, t * TQ:(t + 1) * TQ]
        yk = jnp.dot(h, wk_ref[g], preferred_element_type=F32)
        for c in range(d // LANES):
            yc = yk[:, c * LANES:(c + 1) * LANES]
            y2 = yc * yc
            s_lo = jnp.sum(jnp.where(lo, y2, 0.0), axis=-1, keepdims=True)
            s_hi = jnp.sum(jnp.where(lo, 0.0, y2), axis=-1, keepdims=True)
            ms = jnp.where(lo, s_lo, s_hi) * (1.0 / HEAD_DIM)
            k_ref[g, :, c * LANES:(c + 1) * LANES] = (
                (yc * lax.rsqrt(ms + RMS_EPS)) * kg_ref[...]).astype(BF16)
        yv = lax.dot_general(wv_ref[g], h, nt, preferred_element_type=F32)
        yvb = yv.reshape(d // LANES, LANES, tm).astype(BF16)
        for t in range(nsub):
            vt_ref[g, :, t] = yvb[:, :, t * TQ:(t + 1) * TQ]


def _qkv(x, mod, layer, gain, wq, wk, wv, qg, kg, *, seq, tm):
    t, d = x.shape
    groups = wq.shape[0]
    bsz = t // seq
    nblk = d // LANES
    row = lambda i: layer * bsz + (i * tm) // seq
    tshape = (groups, nblk, t // TQ, LANES, TQ)
    return pl.pallas_call(
        _qkv_kernel,
        out_shape=(jax.ShapeDtypeStruct(tshape, BF16),
                   jax.ShapeDtypeStruct((groups, t, d), BF16),
                   jax.ShapeDtypeStruct(tshape, BF16)),
        grid=(t // tm,),
        in_specs=[pl.BlockSpec((tm, d), lambda i: (i, 0)),
                  pl.BlockSpec((1, d), lambda i: (0, 0)),
                  pl.BlockSpec((1, 1, d), lambda i: (row(i), 0, 0)),
                  pl.BlockSpec((1, 1, d), lambda i: (row(i), 0, 1)),
                  _resident(wq.shape), _resident(wk.shape), _resident(wv.shape),
                  pl.BlockSpec((HEAD_DIM, 1), lambda i: (0, 0)),
                  pl.BlockSpec((1, LANES), lambda i: (0, 0))],
        out_specs=(pl.BlockSpec((groups, nblk, tm // TQ, LANES, TQ), lambda i: (0, 0, i, 0, 0)),
                   pl.BlockSpec((groups, tm, d), lambda i: (0, i, 0)),
                   pl.BlockSpec((groups, nblk, tm // TQ, LANES, TQ), lambda i: (0, 0, i, 0, 0))),
        compiler_params=_params("arbitrary"),
        name=f"qkv_proj_g{groups}",
    )(x, gain.reshape(1, d), mod, mod, wq, wk, wv, qg, kg)


def _attn_kernel(*refs, nds, diff, lambda_init):
    nseg = len(nds)
    segs = [refs[4 * s:4 * s + 4] for s in range(nseg)]
    pos = 4 * nseg
    if diff:
        lamv_ref, subg_ref = refs[pos], refs[pos + 1]
        pos += 2
    o_ref, s_ref = refs[pos], refs[pos + 1]
    nq = segs[0][0].shape[2]
    top = lax.broadcasted_iota(jnp.int32, (LANES, TQ), 0) < HEAD_DIM
    if diff:
        lv = lamv_ref[...]
        lam = (jnp.exp(jnp.sum(lv[0:1] * lv[1:2], axis=-1, keepdims=True))
               - jnp.exp(jnp.sum(lv[2:3] * lv[3:4], axis=-1, keepdims=True)) + lambda_init)

    ones_rows = jnp.ones((BF16_ROWS, TQ), BF16)
    state = {}
    slot = {}

    def tiles_of(i):
        return [(s, kj) for s, nd in enumerate(nds) for kj in range(max(i - nd + 1, 0), i + 1)]

    def scores(i, n):
        st_i = state.setdefault(i, {"m": [None, None], "acc": [None, None], "q": {}})
        s, kj = tiles_of(i)[n]
        qt_ref, k_ref, _, b_ref = segs[s]
        if s not in st_i["q"]:
            qt = qt_ref[0, 0, i]
            zero = jnp.zeros_like(qt)
            st_i["q"][s] = (jnp.where(top, qt, zero), jnp.where(top, zero, qt))
        kk = k_ref[0, kj * TQ:(kj + 1) * TQ, :]
        for st in range(2):
            sc = jnp.dot(kk, st_i["q"][s][st], preferred_element_type=F32) + b_ref[i - kj, st]
            s_ref[slot[(i, n)], st] = sc
            cm = jnp.max(sc, axis=0, keepdims=True)
            st_i["m"][st] = cm if st_i["m"][st] is None else jnp.maximum(st_i["m"][st], cm)

    def weigh(i, n):
        st_i = state[i]
        s, kj = tiles_of(i)[n]
        vt1 = jnp.concatenate([segs[s][2][0, 0, kj], ones_rows], axis=0)
        for st in range(2):
            p = jnp.exp2(s_ref[slot[(i, n)], st] - st_i["m"][st])
            pv = jnp.dot(vt1, p.astype(BF16), preferred_element_type=F32)
            st_i["acc"][st] = pv if st_i["acc"][st] is None else st_i["acc"][st] + pv

    def finish(i):
        acc = state.pop(i)["acc"]
        oa = acc[0][:LANES] * (1.0 / acc[0][LANES:LANES + 1])
        ob = acc[1][:LANES] * (1.0 / acc[1][LANES:LANES + 1])
        if diff:
            ot = oa - lam * ob
            ms = jnp.mean(ot * ot, axis=0, keepdims=True)
            ot = ((ot * lax.rsqrt(ms + RMS_EPS)) * subg_ref[...]) * (1.0 - lambda_init)
        else:
            ot = jnp.where(top, oa, ob)
        o_ref[i * TQ:(i + 1) * TQ, :] = ot.T.astype(BF16)

    order = list(range(0, nq, 2)) + list(range(nq - 1 - nq % 2, 0, -2))
    flat = [(i, n) for i in order for n in range(len(tiles_of(i)))]
    last1 = {i: max(t for t, (ii, _) in enumerate(flat) if ii == i) for i in order}
    assert 2 * max(len(tiles_of(i)) for i in order) + PASS1_MARGIN <= s_ref.shape[0]
    for t, (i, n) in enumerate(flat):
        slot[(i, n)] = t % s_ref.shape[0]
    issued = 0
    for t, (i, n) in enumerate(flat):
        n_cur = len(tiles_of(i))
        nxt = order.index(i) + 1
        n_nxt = len(tiles_of(order[nxt])) if nxt < len(order) else 0
        ahead = PASS1_MARGIN + -(-(n + 1) * n_nxt // n_cur)
        need = min(last1[i] + ahead, len(flat) - 1)
        while issued <= need:
            scores(*flat[issued])
            issued += 1
        weigh(i, n)
        if t == last1[i]:
            finish(i)


def _attention(qt, k, vt, biases, *, seq, diff, lambda_init=0.0, lamv=None, subg=None):
    groups, nblk, ntile, _, _ = qt.shape
    t, d = k.shape[1], k.shape[2]
    bsz = t // seq
    nq = seq // TQ
    nds = tuple(b.shape[0] for b in biases)
    in_specs, args = [], []
    for g in range(groups):
        in_specs += [
            pl.BlockSpec((1, 1, nq, LANES, TQ), lambda j, b, g=g: (g, j, b, 0, 0)),
            pl.BlockSpec((1, seq, LANES), lambda j, b, g=g: (g, b, j)),
            pl.BlockSpec((1, 1, nq, LANES, TQ), lambda j, b, g=g: (g, j, b, 0, 0)),
            pl.BlockSpec((nds[g], 2, TQ, TQ), lambda j, b: (0, j, 0, 0)),
        ]
        args += [qt, k, vt, biases[g]]
    if diff:
        in_specs += [pl.BlockSpec(lamv.shape, lambda j, b: (0, 0)),
                     pl.BlockSpec((LANES, 1), lambda j, b: (0, 0))]
        args += [lamv, subg]
    return pl.pallas_call(
        functools.partial(_attn_kernel, nds=nds, diff=diff, lambda_init=lambda_init),
        out_shape=jax.ShapeDtypeStruct((t, d), BF16),
        grid=(nblk, bsz),
        in_specs=in_specs,
        out_specs=pl.BlockSpec((seq, LANES), lambda j, b: (b, j)),
        scratch_shapes=[pltpu.VMEM((2 * sum(min(nd, nq) for nd in nds) + PASS1_MARGIN, 2, TQ, TQ), F32)],
        compiler_params=_params("arbitrary", "arbitrary"),
        name="diff_attention" if diff else "dilated_attention",
    )(*args)


def _lru_kernel(x_ref, g_ref, sh_ref, sc_ref, win_ref, cw_ref, cb_ref, wx_ref, bx_ref,
                wa_ref, ba_ref, ap_ref, o_ref, uext_ref, h_ref):
    ts, d = x_ref.shape

    @pl.when(pl.program_id(1) == 0)
    def _():
        uext_ref[0:SUBLANES, :] = jnp.zeros((SUBLANES, d), F32)
        h_ref[...] = jnp.zeros_like(h_ref)

    h = _norm_mod(x_ref[...], g_ref[...], sh_ref[0], sc_ref[0]).astype(BF16)
    yu = jnp.dot(h, win_ref[...], preferred_element_type=F32)
    y = yu[:, :d]
    u = yu[:, d:]
    uext_ref[SUBLANES:SUBLANES + ts, :] = u
    conv = cb_ref[...] + u * cw_ref[0:1, :]
    for tap in range(1, CONV_WIDTH):
        conv = conv + uext_ref[SUBLANES - tap:SUBLANES - tap + ts, :] * cw_ref[tap:tap + 1, :]
    uext_ref[0:SUBLANES, :] = u[ts - SUBLANES:ts, :]
    uc = conv
    ucb = uc.astype(BF16)
    nblk = wx_ref.shape[0]
    bw = d // nblk
    gx = jnp.concatenate([jnp.dot(ucb[:, n * bw:(n + 1) * bw], wx_ref[n], preferred_element_type=F32)
                          for n in range(nblk)], axis=1) + bx_ref[...]
    ga = jnp.concatenate([jnp.dot(ucb[:, n * bw:(n + 1) * bw], wa_ref[n], preferred_element_type=F32)
                          for n in range(nblk)], axis=1) + ba_ref[...]
    gate_x = jax.nn.sigmoid(gx)
    gate_a = jax.nn.sigmoid(ga)
    z = -ap_ref[...]
    softplus = jnp.maximum(z, 0.0) + jnp.log1p(jnp.exp(-jnp.abs(z)))
    log_a = (-LRU_C * gate_a) * softplus
    a = jnp.exp(log_a)
    b = jnp.sqrt(-jnp.tanh(log_a) * (1.0 + a * a)) * (gate_x * uc)
    ngrp = ts // SUBLANES
    a = a.reshape(ngrp, SUBLANES, d)
    b = b.reshape(ngrp, SUBLANES, d)
    sub = lax.broadcasted_iota(jnp.int32, (ngrp, SUBLANES, d), 1)
    step = 1
    while step < SUBLANES:
        keep = sub >= step
        a_prev = pltpu.roll(a, step, 1)
        b_prev = pltpu.roll(b, step, 1)
        b = jnp.where(keep, a * b_prev, 0.0) + b
        a = jnp.where(keep, a * a_prev, a)
        step *= 2
    carry = h_ref[...]
    groups = []
    for r in range(ngrp):
        hg = a[r] * carry + b[r]
        groups.append(hg)
        carry = hg[SUBLANES - 1:SUBLANES, :]
    hs = jnp.concatenate(groups, axis=0)
    h_ref[...] = carry
    gelu = 0.5 * y * (1.0 + jnp.tanh(math.sqrt(2.0 / math.pi) * (y + 0.044715 * (y * y * y))))
    o_ref[...] = (hs * gelu).astype(BF16)


def _lru(x, mod, layer, gain, w_in, conv_w, conv_b, w_x, b_x, w_a, b_a, a_param, *, seq, ts):
    t, d = x.shape
    bsz = t // seq
    ns = seq // ts
    vec = lambda v: v.reshape(1, d)
    return pl.pallas_call(
        _lru_kernel,
        out_shape=jax.ShapeDtypeStruct((t, d), BF16),
        grid=(bsz, ns),
        in_specs=[pl.BlockSpec((ts, d), lambda b, s: (b * ns + s, 0)),
                  pl.BlockSpec((1, d), lambda b, s: (0, 0)),
                  pl.BlockSpec((1, 1, d), lambda b, s: (layer * bsz + b, 0, 0)),
                  pl.BlockSpec((1, 1, d), lambda b, s: (layer * bsz + b, 0, 1)),
                  _resident(w_in.shape),
                  pl.BlockSpec((CONV_WIDTH, d), lambda b, s: (0, 0)),
                  pl.BlockSpec((1, d), lambda b, s: (0, 0)),
                  _resident(w_x.shape),
                  pl.BlockSpec((1, d), lambda b, s: (0, 0)),
                  _resident(w_a.shape),
                  pl.BlockSpec((1, d), lambda b, s: (0, 0)),
                  pl.BlockSpec((1, d), lambda b, s: (0, 0))],
        out_specs=pl.BlockSpec((ts, d), lambda b, s: (b * ns + s, 0)),
        scratch_shapes=[pltpu.VMEM((ts + SUBLANES, d), F32), pltpu.VMEM((1, d), F32)],
        compiler_params=_params("arbitrary", "arbitrary"),
        name="rglru_mixer",
    )(x, vec(gain), mod, mod, w_in, conv_w, vec(conv_b), w_x, vec(b_x), w_a, vec(b_a), vec(a_param))


def _post_mlp_kernel(x_ref, y_ref, wo_ref, g1_ref, g_ref, sh_ref, sc_ref, g2_ref, w1_ref, w2_ref,
                     o_ref, *, ff_chunk):
    x1 = x_ref[...] + g1_ref[0] * jnp.dot(y_ref[...], wo_ref[...], preferred_element_type=F32)
    h = _norm_mod(x1, g_ref[...], sh_ref[0], sc_ref[0]).astype(BF16)
    dff = w1_ref.shape[1]
    acc = None
    for c in range(dff // ff_chunk):
        a = jnp.dot(h, w1_ref[:, c * ff_chunk:(c + 1) * ff_chunk], preferred_element_type=F32)
        a = jnp.maximum(a, 0.0)
        a = (a * a).astype(BF16)
        part = jnp.dot(a, w2_ref[c * ff_chunk:(c + 1) * ff_chunk, :], preferred_element_type=F32)
        acc = part if acc is None else acc + part
    o_ref[...] = x1 + g2_ref[0] * acc


def _post_mlp(x, y, mod, layer, w_o, gain, w1, w2, *, seq, tm, ff_chunk):
    t, d = x.shape
    bsz = t // seq
    row = lambda i: layer * bsz + (i * tm) // seq
    part = lambda p: pl.BlockSpec((1, 1, d), lambda i: (row(i), 0, p))
    return pl.pallas_call(
        functools.partial(_post_mlp_kernel, ff_chunk=ff_chunk),
        out_shape=jax.ShapeDtypeStruct((t, d), F32),
        grid=(t // tm,),
        in_specs=[pl.BlockSpec((tm, d), lambda i: (i, 0)),
                  pl.BlockSpec((tm, d), lambda i: (i, 0)),
                  _resident(w_o.shape),
                  part(2),
                  pl.BlockSpec((1, d), lambda i: (0, 0)),
                  part(3), part(4), part(5),
                  _resident(w1.shape), _resident(w2.shape)],
        out_specs=pl.BlockSpec((tm, d), lambda i: (i, 0)),
        compiler_params=_params("arbitrary"),
        name="outproj_mlp",
    )(x, y, w_o, mod, gain.reshape(1, d), mod, mod, mod, w1, w2)


def kernel(x, c, rel_bias, w_ada, b_ada, norm_mix, norm_mlp, mlp_w1, mlp_w2, da_w_qkv, da_w_o, da_q_gain, da_k_gain, da_lam_q1, da_lam_k1, da_lam_q2, da_lam_k2, da_sub_gain, lru_w_in, lru_conv_w, lru_conv_b, lru_w_x, lru_b_x, lru_w_a, lru_b_a, lru_a_param, lru_w_out, dil_w_qkv, dil_w_o, dil_q_gain, dil_k_gain):
    bsz, seq, d = x.shape
    depth = w_ada.shape[0]
    t = bsz * seq
    nq = seq // TQ
    assert seq % TQ == 0 and d % LANES == 0

    mod = _ada(c, w_ada, b_ada).reshape(depth * bsz, 1, 6 * d)

    ncols = rel_bias.shape[1]
    rb_ext = jnp.concatenate(
        [rel_bias.T.astype(F32), jnp.full((ncols, 1), NEG_INF, F32),
         jnp.zeros((ncols, BUCKET_PAD - NUM_BUCKETS - 1), F32)], axis=1)
    kinds = [i % N_MIXERS for i in range(depth)]
    da_bias = _bias_tiles(rb_ext, nq, None, None) if 0 in kinds else None
    dil_bias = None
    if 2 in kinds:
        dil_bias = [_bias_tiles(rb_ext, min((w + TQ - 1) // TQ, nq - 1) + 1, r, w)
                    for (w, r) in DIL_PATTERNS]

    q_scale = HEAD_DIM ** -0.5 * LOG2E
    xf = x.reshape(t, d)
    for i in range(depth):
        kind, slot = kinds[i], i // N_MIXERS
        if kind == 0:
            w = da_w_qkv[slot]
            wq = w[:, :d].T.astype(BF16)[None]
            wk = w[:, d:2 * d].astype(BF16)[None]
            wv = w[:, 2 * d:].T.astype(BF16)[None]
            qg = (da_q_gain[slot] * q_scale).reshape(HEAD_DIM, 1)
            kg = jnp.tile(da_k_gain[slot], 2).reshape(1, LANES)
            qt, k, vt = _qkv(xf, mod, i, norm_mix[i], wq, wk, wv, qg, kg, seq=seq, tm=2 * TQ)
            lambda_init = 0.8 - 0.6 * math.exp(-0.3 * i)
            lamv = jnp.stack([da_lam_q1[slot], da_lam_k1[slot], da_lam_q2[slot], da_lam_k2[slot]]).astype(F32)
            y = _attention(qt, k, vt, [da_bias], seq=seq, diff=True, lambda_init=lambda_init,
                           lamv=lamv, subg=da_sub_gain[slot].reshape(LANES, 1))
            w_o = da_w_o[slot]
        elif kind == 1:
            y = _lru(xf, mod, i, norm_mix[i], lru_w_in[slot].astype(BF16), lru_conv_w[slot],
                     lru_conv_b[slot], lru_w_x[slot].astype(BF16), lru_b_x[slot],
                     lru_w_a[slot].astype(BF16), lru_b_a[slot], lru_a_param[slot], seq=seq, ts=TQ)
            w_o = lru_w_out[slot]
        else:
            w = dil_w_qkv[slot].reshape(d, len(DIL_PATTERNS), 3, d)
            wq = jnp.transpose(w[:, :, 0, :], (1, 2, 0)).astype(BF16)
            wk = jnp.transpose(w[:, :, 1, :], (1, 0, 2)).astype(BF16)
            wv = jnp.transpose(w[:, :, 2, :], (1, 2, 0)).astype(BF16)
            qg = (dil_q_gain[slot] * q_scale).reshape(HEAD_DIM, 1)
            kg = jnp.tile(dil_k_gain[slot], 2).reshape(1, LANES)
            qt, k, vt = _qkv(xf, mod, i, norm_mix[i], wq, wk, wv, qg, kg, seq=seq, tm=TQ)
            y = _attention(qt, k, vt, dil_bias, seq=seq, diff=False)
            w_o = dil_w_o[slot]
        xf = _post_mlp(xf, y, mod, i, w_o.astype(BF16), norm_mlp[i], mlp_w1[i].astype(BF16),
                       mlp_w2[i].astype(BF16), seq=seq, tm=2 * TQ, ff_chunk=1024)
    return xf.reshape(bsz, seq, d)
```

```python
import functools
import math

import numpy as np
import jax
import jax.numpy as jnp
from jax import lax
from jax.experimental import pallas as pl
from jax.experimental.pallas import tpu as pltpu

F32 = jnp.float32
BF16 = jnp.bfloat16

RMS_EPS = 1e-6
NEG_INF = -1e30
NUM_BUCKETS = 32
REL_MAX_DISTANCE = 2048
HEAD_DIM = 64
N_MIXERS = 3
CONV_WIDTH = 4
LRU_C = 8.0
DIL_PATTERNS = ((128, 1), (512, 4), (2048, 16))
LOG2E = math.log2(math.e)
MASKED_BUCKET = NUM_BUCKETS
BUCKET_PAD = 64

LANES = 128
SUBLANES = 8
BF16_ROWS = 16
MXU_WIDTH = 256
VMEM_LIMIT_BYTES = 56 * 1024 * 1024

TQ = MXU_WIDTH
SCORE_LEAD = 2


def _params(*sem):
    return pltpu.CompilerParams(dimension_semantics=sem, vmem_limit_bytes=VMEM_LIMIT_BYTES)


def _resident(shape):
    nd = len(shape)
    return pl.BlockSpec(shape, lambda *_: (0,) * nd, pipeline_mode=pl.Buffered(1))


def _norm_mod(x, gain, shift, scale):
    ms = jnp.mean(x * x, axis=-1, keepdims=True)
    y = (x * lax.rsqrt(ms + RMS_EPS)) * gain
    return y * (1.0 + scale) + shift


def _ada_kernel(c_ref, w_ref, b_ref, o_ref):
    c = c_ref[...]
    a = (c * jax.nn.sigmoid(c)).astype(BF16)
    o_ref[0] = jnp.dot(a, w_ref[0].astype(BF16), preferred_element_type=F32) + b_ref[0]


def _ada(c, w_ada, b_ada):
    depth, d, n = w_ada.shape
    bsz = c.shape[0]
    tn = n // 4
    return pl.pallas_call(
        _ada_kernel,
        out_shape=jax.ShapeDtypeStruct((depth, bsz, n), F32),
        grid=(depth, n // tn),
        in_specs=[pl.BlockSpec((bsz, d), lambda l, j: (0, 0)),
                  pl.BlockSpec((1, d, tn), lambda l, j: (l, 0, j)),
                  pl.BlockSpec((1, 1, tn), lambda l, j: (l, 0, j))],
        out_specs=pl.BlockSpec((1, bsz, tn), lambda l, j: (l, 0, j)),
        compiler_params=_params("arbitrary", "arbitrary"),
        name="ada_mod",
    )(c, w_ada, b_ada.reshape(depth, 1, n))


def _t5_bucket_np(n):
    n = np.maximum(n, 0)
    max_exact = NUM_BUCKETS // 2
    nf = np.maximum(n, max_exact).astype(np.float32)
    large = max_exact + (np.log(nf / np.float32(max_exact)) / np.float32(math.log(REL_MAX_DISTANCE / max_exact))
                         * np.float32(NUM_BUCKETS - max_exact)).astype(np.int32)
    large = np.minimum(large, NUM_BUCKETS - 1)
    return np.where(n < max_exact, n, large)


def _bucket_tiles(nd, dilation, window, class_len):
    key = np.arange(TQ)[:, None]
    qry = np.arange(TQ)[None, :]
    tiles = []
    for delta in range(nd):
        steps = TQ * delta + qry - key
        valid = steps >= 0
        if window is not None:
            valid &= steps <= window
        if class_len < TQ:
            valid &= (key // class_len == qry // class_len) & (delta == 0)
        bucket = _t5_bucket_np(steps * dilation)
        tiles.append(np.where(valid, bucket, MASKED_BUCKET).reshape(1, TQ * TQ))
    return np.stack(tiles).astype(np.int32)


def _bias_kernel(bk_ref, rb_ref, o_ref):
    rb = rb_ref[...] * LOG2E
    hi = rb.astype(BF16)
    r1 = rb - hi.astype(F32)
    mid = r1.astype(BF16)
    low = (r1 - mid.astype(F32)).astype(BF16)
    n = o_ref.shape[2]
    chunk = 8192
    for c in range(n // chunk):
        bk = bk_ref[0, :, c * chunk:(c + 1) * chunk]
        onehot = lax.broadcasted_iota(jnp.int32, (BUCKET_PAD, chunk), 0) == bk
        oh = jnp.where(onehot, 1.0, 0.0).astype(BF16)
        o_ref[0, :, c * chunk:(c + 1) * chunk] = (
            jnp.dot(hi, oh, preferred_element_type=F32)
            + jnp.dot(mid, oh, preferred_element_type=F32)
            + jnp.dot(low, oh, preferred_element_type=F32))


def _bias_tiles(rb_ext, nd, dilation, window, class_len):
    buckets = jnp.asarray(_bucket_tiles(nd, dilation, window, class_len))
    ncols = rb_ext.shape[0]
    out = pl.pallas_call(
        _bias_kernel,
        out_shape=jax.ShapeDtypeStruct((nd, ncols, TQ * TQ), F32),
        grid=(nd,),
        in_specs=[pl.BlockSpec((1, 1, TQ * TQ), lambda i: (i, 0, 0)),
                  pl.BlockSpec((ncols, BUCKET_PAD), lambda i: (0, 0))],
        out_specs=pl.BlockSpec((1, ncols, TQ * TQ), lambda i: (i, 0, 0)),
        compiler_params=_params("arbitrary"),
        name="rel_bias_tiles",
    )(buckets, rb_ext)
    return out.reshape(nd, ncols, TQ, TQ)


def _class_major_view(x, seq, dilation, tm):
    t, d = x.shape
    class_len = seq // dilation
    view = x.reshape(t // dilation, dilation * d)
    steps_per_batch = seq // tm
    if class_len >= tm:
        assert class_len % tm == 0
        per_class = class_len // tm
        index = lambda i: ((i // steps_per_batch) * per_class + (i % steps_per_batch) % per_class,
                           (i % steps_per_batch) // per_class)
        return view, (tm, d), index, 1
    assert tm % class_len == 0 and class_len % SUBLANES == 0
    classes = tm // class_len
    return view, (class_len, classes * d), lambda i: (i // steps_per_batch, i % steps_per_batch), classes


def _qkv_kernel(*refs, classes):
    groups = len(classes)
    x_refs = refs[:groups]
    g_ref, sh_ref, sc_ref, wq_ref, wk_ref, wv_ref, qg_ref, kg_ref, qt_ref, k_ref, vt_ref = refs[groups:]
    d = g_ref.shape[1]
    tm = x_refs[0].shape[0] * classes[0]
    nt = (((1,), (1,)), ((), ()))
    lo = lax.broadcasted_iota(jnp.int32, (tm, LANES), 1) < HEAD_DIM
    nsub = tm // TQ
    for g in range(groups):
        x = x_refs[g][...]
        if classes[g] > 1:
            x = jnp.concatenate([x[:, c * d:(c + 1) * d] for c in range(classes[g])], axis=0)
        h = _norm_mod(x, g_ref[...], sh_ref[0], sc_ref[0]).astype(BF16)
        yq = lax.dot_general(wq_ref[g], h, nt, preferred_element_type=F32)
        y3 = yq.reshape(d // HEAD_DIM, HEAD_DIM, tm)
        ms = jnp.mean(y3 * y3, axis=1, keepdims=True)
        y3 = (y3 * lax.rsqrt(ms + RMS_EPS)) * qg_ref[...][None]
        yb = y3.reshape(d // LANES, LANES, tm).astype(BF16)
        for t in range(nsub):
            qt_ref[g, :, t] = yb[:, :, t * TQ:(t + 1) * TQ]
        yk = jnp.dot(h, wk_ref[g], preferred_element_type=F32)
        for c in range(d // LANES):
            yc = yk[:, c * LANES:(c + 1) * LANES]
            y2 = yc * yc
            s_lo = jnp.sum(jnp.where(lo, y2, 0.0), axis=-1, keepdims=True)
            s_hi = jnp.sum(jnp.where(lo, 0.0, y2), axis=-1, keepdims=True)
            ms = jnp.where(lo, s_lo, s_hi) * (1.0 / HEAD_DIM)
            k_ref[g, :, c * LANES:(c + 1) * LANES] = (
                (yc * lax.rsqrt(ms + RMS_EPS)) * kg_ref[...]).astype(BF16)
        yv = lax.dot_general(wv_ref[g], h, nt, preferred_element_type=F32)
        yvb = yv.reshape(d // LANES, LANES, tm).astype(BF16)
        for t in range(nsub):
            vt_ref[g, :, t] = yvb[:, :, t * TQ:(t + 1) * TQ]


def _qkv(x, mod, layer, gain, wq, wk, wv, qg, kg, *, seq, tm, dilations):
    t, d = x.shape
    groups = wq.shape[0]
    bsz = t // seq
    nblk = d // LANES
    row = lambda i: layer * bsz + (i * tm) // seq
    tshape = (groups, nblk, t // TQ, LANES, TQ)
    views = [_class_major_view(x, seq, r, tm) for r in dilations]
    return pl.pallas_call(
        functools.partial(_qkv_kernel, classes=tuple(v[3] for v in views)),
        out_shape=(jax.ShapeDtypeStruct(tshape, BF16),
                   jax.ShapeDtypeStruct((groups, t, d), BF16),
                   jax.ShapeDtypeStruct(tshape, BF16)),
        grid=(t // tm,),
        in_specs=[pl.BlockSpec(v[1], v[2]) for v in views] + [
                  pl.BlockSpec((1, d), lambda i: (0, 0)),
                  pl.BlockSpec((1, 1, d), lambda i: (row(i), 0, 0)),
                  pl.BlockSpec((1, 1, d), lambda i: (row(i), 0, 1)),
                  _resident(wq.shape), _resident(wk.shape), _resident(wv.shape),
                  pl.BlockSpec((HEAD_DIM, 1), lambda i: (0, 0)),
                  pl.BlockSpec((1, LANES), lambda i: (0, 0))],
        out_specs=(pl.BlockSpec((groups, nblk, tm // TQ, LANES, TQ), lambda i: (0, 0, i, 0, 0)),
                   pl.BlockSpec((groups, tm, d), lambda i: (0, i, 0)),
                   pl.BlockSpec((groups, nblk, tm // TQ, LANES, TQ), lambda i: (0, 0, i, 0, 0))),
        compiler_params=_params("arbitrary"),
        name=f"qkv_proj_g{groups}",
    )(*[v[0] for v in views], gain.reshape(1, d), mod, mod, wq, wk, wv, qg, kg)


def _diff_plan(nq):
    return [(0, u, [(kj, u - kj) for kj in range(u + 1)], [(0, TQ, u * TQ, 1)]) for u in range(nq)]


def _dilated_plan(seq, patterns):
    plan = []
    for g, (window, r) in enumerate(patterns):
        class_len = seq // r
        assert window // r <= TQ
        for u in range(seq // TQ):
            if class_len >= TQ:
                per_class = class_len // TQ
                c, lt = u // per_class, u % per_class
                tiles = ([(u - 1, 1)] if lt > 0 else []) + [(u, 0)]
                stores = [(0, TQ, lt * TQ * r + c, r)]
            else:
                classes = TQ // class_len
                tiles = [(u, 0)]
                stores = [(h * class_len, class_len, u * classes + h, r) for h in range(classes)]
            plan.append((g, u, tiles, stores))
    return plan


def _attn_kernel(*refs, plan, nseg, diff, lambda_init):
    segs = [refs[4 * s:4 * s + 4] for s in range(nseg)]
    pos = 4 * nseg
    if diff:
        lamv_ref, subg_ref = refs[pos], refs[pos + 1]
        pos += 2
    o_ref, s_ref = refs[pos], refs[pos + 1]
    if not diff:
        og_ref, lse_ref = refs[pos + 2], refs[pos + 3]
    top = lax.broadcasted_iota(jnp.int32, (LANES, TQ), 0) < HEAD_DIM
    if diff:
        lv = lamv_ref[...]
        lam = (jnp.exp(jnp.sum(lv[0:1] * lv[1:2], axis=-1, keepdims=True))
               - jnp.exp(jnp.sum(lv[2:3] * lv[3:4], axis=-1, keepdims=True)) + lambda_init)

    ones_rows = jnp.ones((BF16_ROWS, TQ), BF16)
    state = {}
    slot = {}

    def tiles_of(i):
        return plan[i][2]

    def rows(ref, g, start, n, stride):
        return ref.at[g, pl.ds(start, n, stride=stride) if stride > 1 else pl.ds(start, n), :]

    def scores(i, n):
        st_i = state.setdefault(i, {"m": [None, None], "mh": ([], []), "acc": [None, None], "q": None})
        g, u, tiles, _ = plan[i]
        kj, bidx = tiles[n]
        qt_ref, k_ref, _, b_ref = segs[g]
        if st_i["q"] is None:
            qt = qt_ref[0, 0, u]
            zero = jnp.zeros_like(qt)
            st_i["q"] = (jnp.where(top, qt, zero), jnp.where(top, zero, qt))
        kk = k_ref[0, kj * TQ:(kj + 1) * TQ, :]
        for st in range(2):
            sc = jnp.dot(kk, st_i["q"][st], preferred_element_type=F32) + b_ref[bidx, st]
            s_ref[slot[(i, n)], st] = sc
            cm = jnp.max(sc, axis=0, keepdims=True)
            st_i["m"][st] = cm if st_i["m"][st] is None else jnp.maximum(st_i["m"][st], cm)
            st_i["mh"][st].append(st_i["m"][st])

    def weigh(i, n):
        st_i = state[i]
        g, _, tiles, _ = plan[i]
        vt1 = jnp.concatenate([segs[g][2][0, 0, tiles[n][0]], ones_rows], axis=0)
        for st in range(2):
            mh = st_i["mh"][st]
            p = jnp.exp2(s_ref[slot[(i, n)], st] - mh[n])
            pv = jnp.dot(vt1, p.astype(BF16), preferred_element_type=F32)
            st_i["acc"][st] = pv if n == 0 else jnp.exp2(mh[n - 1] - mh[n]) * st_i["acc"][st] + pv

    def finish(i):
        g, _, _, stores = plan[i]
        st_i = state.pop(i)
        acc, m = st_i["acc"], st_i["m"]
        la, lb = acc[0][LANES:LANES + 1], acc[1][LANES:LANES + 1]
        oa = acc[0][:LANES] * (1.0 / la)
        ob = acc[1][:LANES] * (1.0 / lb)
        if diff:
            ot = oa - lam * ob
            ms = jnp.mean(ot * ot, axis=0, keepdims=True)
            ot = ((ot * lax.rsqrt(ms + RMS_EPS)) * subg_ref[...]) * (1.0 - lambda_init)
            for row0, n, dst, stride in stores:
                assert stride == 1
                o_ref[dst:dst + n, :] = ot.T[row0:row0 + n].astype(BF16)
        else:
            o_rows = jnp.where(top, oa, ob).T
            lse_rows = jnp.where(top, m[0] + jnp.log2(la), m[1] + jnp.log2(lb)).T
            for row0, n, dst, stride in stores:
                rows(og_ref, g, dst, n, stride)[...] = o_rows[row0:row0 + n]
                rows(lse_ref, g, dst, n, stride)[...] = lse_rows[row0:row0 + n]

    def chunks_of(i):
        return {(dst + t * stride) // TQ for _, n, dst, stride in plan[i][3] for t in range(n)}

    nchunk = o_ref.shape[0] // TQ
    if diff:
        by_size = sorted(range(len(plan)), key=lambda i: len(tiles_of(i)))
        order = by_size[0::2] + by_size[1::2][::-1]
    else:
        order = []
        for c in range(nchunk):
            order += [i for i in range(len(plan)) if c in chunks_of(i) and i not in order]
    needs = {c: {i for i in range(len(plan)) if c in chunks_of(i)} for c in range(nchunk)}
    finished, merged = set(), set()

    def merge_ready():
        for c in range(nchunk):
            if c in merged or not needs[c] <= finished:
                continue
            merged.add(c)
            r0 = c * TQ
            lses = [lse_ref[g, r0:r0 + TQ, :] for g in range(nseg)]
            top_lse = functools.reduce(jnp.maximum, lses)
            ws = [jnp.exp2(l - top_lse) for l in lses]
            num = sum(w * og_ref[g, r0:r0 + TQ, :] for g, w in enumerate(ws))
            o_ref[r0:r0 + TQ, :] = (num * (1.0 / sum(ws))).astype(BF16)

    flat = [(i, n) for i in order for n in range(len(tiles_of(i)))]
    assert s_ref.shape[0] > SCORE_LEAD
    for t, (i, n) in enumerate(flat):
        slot[(i, n)] = t % s_ref.shape[0]
    issued = 0
    for t, (i, n) in enumerate(flat):
        while issued <= min(t + SCORE_LEAD, len(flat) - 1):
            scores(*flat[issued])
            issued += 1
        weigh(i, n)
        if n == len(tiles_of(i)) - 1:
            finish(i)
            finished.add(i)
            if not diff:
                merge_ready()
    assert diff or len(merged) == nchunk


def _attention(qt, k, vt, biases, plan, *, seq, diff, lambda_init=0.0, lamv=None, subg=None):
    groups, nblk, ntile, _, _ = qt.shape
    t, d = k.shape[1], k.shape[2]
    bsz = t // seq
    nq = seq // TQ
    nds = tuple(b.shape[0] for b in biases)
    scratch = [pltpu.VMEM((SCORE_LEAD + 2, 2, TQ, TQ), F32)]
    if not diff:
        scratch += [pltpu.VMEM((groups, seq, LANES), F32), pltpu.VMEM((groups, seq, LANES), F32)]
    in_specs, args = [], []
    for g in range(groups):
        in_specs += [
            pl.BlockSpec((1, 1, nq, LANES, TQ), lambda j, b, g=g: (g, j, b, 0, 0)),
            pl.BlockSpec((1, seq, LANES), lambda j, b, g=g: (g, b, j)),
            pl.BlockSpec((1, 1, nq, LANES, TQ), lambda j, b, g=g: (g, j, b, 0, 0)),
            pl.BlockSpec((nds[g], 2, TQ, TQ), lambda j, b: (0, j, 0, 0)),
        ]
        args += [qt, k, vt, biases[g]]
    if diff:
        in_specs += [pl.BlockSpec(lamv.shape, lambda j, b: (0, 0)),
                     pl.BlockSpec((LANES, 1), lambda j, b: (0, 0))]
        args += [lamv, subg]
    return pl.pallas_call(
        functools.partial(_attn_kernel, plan=plan, nseg=groups, diff=diff, lambda_init=lambda_init),
        out_shape=jax.ShapeDtypeStruct((t, d), BF16),
        grid=(nblk, bsz),
        in_specs=in_specs,
        out_specs=pl.BlockSpec((seq, LANES), lambda j, b: (b, j)),
        scratch_shapes=scratch,
        compiler_params=_params("arbitrary", "arbitrary"),
        name="diff_attention" if diff else "dilated_attention",
    )(*args)


def _lru_kernel(x_ref, g_ref, sh_ref, sc_ref, win_ref, cw_ref, cb_ref, wx_ref, bx_ref,
                wa_ref, ba_ref, ap_ref, o_ref, uext_ref, h_ref):
    ts, d = x_ref.shape

    @pl.when(pl.program_id(1) == 0)
    def _():
        uext_ref[0:SUBLANES, :] = jnp.zeros((SUBLANES, d), F32)
        h_ref[...] = jnp.zeros_like(h_ref)

    h = _norm_mod(x_ref[...], g_ref[...], sh_ref[0], sc_ref[0]).astype(BF16)
    yu = jnp.dot(h, win_ref[...], preferred_element_type=F32)
    y = yu[:, :d]
    u = yu[:, d:]
    uext_ref[SUBLANES:SUBLANES + ts, :] = u
    conv = cb_ref[...] + u * cw_ref[0:1, :]
    for tap in range(1, CONV_WIDTH):
        conv = conv + uext_ref[SUBLANES - tap:SUBLANES - tap + ts, :] * cw_ref[tap:tap + 1, :]
    uext_ref[0:SUBLANES, :] = u[ts - SUBLANES:ts, :]
    uc = conv
    ucb = uc.astype(BF16)
    nblk = wx_ref.shape[0]
    bw = d // nblk
    gx = jnp.concatenate([jnp.dot(ucb[:, n * bw:(n + 1) * bw], wx_ref[n], preferred_element_type=F32)
                          for n in range(nblk)], axis=1) + bx_ref[...]
    ga = jnp.concatenate([jnp.dot(ucb[:, n * bw:(n + 1) * bw], wa_ref[n], preferred_element_type=F32)
                          for n in range(nblk)], axis=1) + ba_ref[...]
    gate_x = jax.nn.sigmoid(gx)
    gate_a = jax.nn.sigmoid(ga)
    z = -ap_ref[...]
    softplus = jnp.maximum(z, 0.0) + jnp.log1p(jnp.exp(-jnp.abs(z)))
    log_a = (-LRU_C * gate_a) * softplus
    a = jnp.exp(log_a)
    b = jnp.sqrt(-jnp.tanh(log_a) * (1.0 + a * a)) * (gate_x * uc)
    ngrp = ts // SUBLANES
    a = a.reshape(ngrp, SUBLANES, d)
    b = b.reshape(ngrp, SUBLANES, d)
    sub = lax.broadcasted_iota(jnp.int32, (ngrp, SUBLANES, d), 1)
    step = 1
    while step < SUBLANES:
        keep = sub >= step
        a_prev = pltpu.roll(a, step, 1)
        b_prev = pltpu.roll(b, step, 1)
        b = jnp.where(keep, a * b_prev, 0.0) + b
        a = jnp.where(keep, a * a_prev, a)
        step *= 2
    carry = h_ref[...]
    groups = []
    for r in range(ngrp):
        hg = a[r] * carry + b[r]
        groups.append(hg)
        carry = hg[SUBLANES - 1:SUBLANES, :]
    hs = jnp.concatenate(groups, axis=0)
    h_ref[...] = carry
    gelu = 0.5 * y * (1.0 + jnp.tanh(math.sqrt(2.0 / math.pi) * (y + 0.044715 * (y * y * y))))
    o_ref[...] = (hs * gelu).astype(BF16)


def _lru(x, mod, layer, gain, w_in, conv_w, conv_b, w_x, b_x, w_a, b_a, a_param, *, seq, ts):
    t, d = x.shape
    bsz = t // seq
    ns = seq // ts
    vec = lambda v: v.reshape(1, d)
    return pl.pallas_call(
        _lru_kernel,
        out_shape=jax.ShapeDtypeStruct((t, d), BF16),
        grid=(bsz, ns),
        in_specs=[pl.BlockSpec((ts, d), lambda b, s: (b * ns + s, 0)),
                  pl.BlockSpec((1, d), lambda b, s: (0, 0)),
                  pl.BlockSpec((1, 1, d), lambda b, s: (layer * bsz + b, 0, 0)),
                  pl.BlockSpec((1, 1, d), lambda b, s: (layer * bsz + b, 0, 1)),
                  _resident(w_in.shape),
                  pl.BlockSpec((CONV_WIDTH, d), lambda b, s: (0, 0)),
                  pl.BlockSpec((1, d), lambda b, s: (0, 0)),
                  _resident(w_x.shape),
                  pl.BlockSpec((1, d), lambda b, s: (0, 0)),
                  _resident(w_a.shape),
                  pl.BlockSpec((1, d), lambda b, s: (0, 0)),
                  pl.BlockSpec((1, d), lambda b, s: (0, 0))],
        out_specs=pl.BlockSpec((ts, d), lambda b, s: (b * ns + s, 0)),
        scratch_shapes=[pltpu.VMEM((ts + SUBLANES, d), F32), pltpu.VMEM((1, d), F32)],
        compiler_params=_params("arbitrary", "arbitrary"),
        name="rglru_mixer",
    )(x, vec(gain), mod, mod, w_in, conv_w, vec(conv_b), w_x, vec(b_x), w_a, vec(b_a), vec(a_param))


def _post_mlp_kernel(x_ref, y_ref, wo_ref, g1_ref, g_ref, sh_ref, sc_ref, g2_ref, w1_ref, w2_ref,
                     o_ref, *, ff_chunk):
    x1 = x_ref[...] + g1_ref[0] * jnp.dot(y_ref[...], wo_ref[...], preferred_element_type=F32)
    h = _norm_mod(x1, g_ref[...], sh_ref[0], sc_ref[0]).astype(BF16)
    dff = w1_ref.shape[1]
    acc = None
    for c in range(dff // ff_chunk):
        a = jnp.dot(h, w1_ref[:, c * ff_chunk:(c + 1) * ff_chunk], preferred_element_type=F32)
        a = jnp.maximum(a, 0.0)
        a = (a * a).astype(BF16)
        part = jnp.dot(a, w2_ref[c * ff_chunk:(c + 1) * ff_chunk, :], preferred_element_type=F32)
        acc = part if acc is None else acc + part
    o_ref[...] = x1 + g2_ref[0] * acc


def _post_mlp(x, y, mod, layer, w_o, gain, w1, w2, *, seq, tm, ff_chunk):
    t, d = x.shape
    bsz = t // seq
    row = lambda i: layer * bsz + (i * tm) // seq
    part = lambda p: pl.BlockSpec((1, 1, d), lambda i: (row(i), 0, p))
    return pl.pallas_call(
        functools.partial(_post_mlp_kernel, ff_chunk=ff_chunk),
        out_shape=jax.ShapeDtypeStruct((t, d), F32),
        grid=(t // tm,),
        in_specs=[pl.BlockSpec((tm, d), lambda i: (i, 0)),
                  pl.BlockSpec((tm, d), lambda i: (i, 0)),
                  _resident(w_o.shape),
                  part(2),
                  pl.BlockSpec((1, d), lambda i: (0, 0)),
                  part(3), part(4), part(5),
                  _resident(w1.shape), _resident(w2.shape)],
        out_specs=pl.BlockSpec((tm, d), lambda i: (i, 0)),
        compiler_params=_params("arbitrary"),
        name="outproj_mlp",
    )(x, y, w_o, mod, gain.reshape(1, d), mod, mod, mod, w1, w2)


def kernel(x, c, rel_bias, w_ada, b_ada, norm_mix, norm_mlp, mlp_w1, mlp_w2, da_w_qkv, da_w_o, da_q_gain, da_k_gain, da_lam_q1, da_lam_k1, da_lam_q2, da_lam_k2, da_sub_gain, lru_w_in, lru_conv_w, lru_conv_b, lru_w_x, lru_b_x, lru_w_a, lru_b_a, lru_a_param, lru_w_out, dil_w_qkv, dil_w_o, dil_q_gain, dil_k_gain):
    bsz, seq, d = x.shape
    depth = w_ada.shape[0]
    t = bsz * seq
    nq = seq // TQ
    assert seq % TQ == 0 and d % LANES == 0

    mod = _ada(c, w_ada, b_ada).reshape(depth * bsz, 1, 6 * d)

    ncols = rel_bias.shape[1]
    rb_ext = jnp.concatenate(
        [rel_bias.T.astype(F32), jnp.full((ncols, 1), NEG_INF, F32),
         jnp.zeros((ncols, BUCKET_PAD - NUM_BUCKETS - 1), F32)], axis=1)
    kinds = [i % N_MIXERS for i in range(depth)]
    da_bias = _bias_tiles(rb_ext, nq, 1, None, seq) if 0 in kinds else None
    dil_bias = None
    if 2 in kinds:
        dil_bias = [_bias_tiles(rb_ext, 2 if seq // r >= 2 * TQ else 1, r, w // r, seq // r)
                    for (w, r) in DIL_PATTERNS]

    q_scale = HEAD_DIM ** -0.5 * LOG2E
    xf = x.reshape(t, d)
    for i in range(depth):
        kind, slot = kinds[i], i // N_MIXERS
        if kind == 0:
            w = da_w_qkv[slot]
            wq = w[:, :d].T.astype(BF16)[None]
            wk = w[:, d:2 * d].astype(BF16)[None]
            wv = w[:, 2 * d:].T.astype(BF16)[None]
            qg = (da_q_gain[slot] * q_scale).reshape(HEAD_DIM, 1)
            kg = jnp.tile(da_k_gain[slot], 2).reshape(1, LANES)
            qt, k, vt = _qkv(xf, mod, i, norm_mix[i], wq, wk, wv, qg, kg, seq=seq, tm=2 * TQ,
                             dilations=(1,))
            lambda_init = 0.8 - 0.6 * math.exp(-0.3 * i)
            lamv = jnp.stack([da_lam_q1[slot], da_lam_k1[slot], da_lam_q2[slot], da_lam_k2[slot]]).astype(F32)
            y = _attention(qt, k, vt, [da_bias], _diff_plan(nq), seq=seq, diff=True, lambda_init=lambda_init,
                           lamv=lamv, subg=da_sub_gain[slot].reshape(LANES, 1))
            w_o = da_w_o[slot]
        elif kind == 1:
            y = _lru(xf, mod, i, norm_mix[i], lru_w_in[slot].astype(BF16), lru_conv_w[slot],
                     lru_conv_b[slot], lru_w_x[slot].astype(BF16), lru_b_x[slot],
                     lru_w_a[slot].astype(BF16), lru_b_a[slot], lru_a_param[slot], seq=seq, ts=TQ)
            w_o = lru_w_out[slot]
        else:
            w = dil_w_qkv[slot].reshape(d, len(DIL_PATTERNS), 3, d)
            wq = jnp.transpose(w[:, :, 0, :], (1, 2, 0)).astype(BF16)
            wk = jnp.transpose(w[:, :, 1, :], (1, 0, 2)).astype(BF16)
            wv = jnp.transpose(w[:, :, 2, :], (1, 2, 0)).astype(BF16)
            qg = (dil_q_gain[slot] * q_scale).reshape(HEAD_DIM, 1)
            kg = jnp.tile(dil_k_gain[slot], 2).reshape(1, LANES)
            qt, k, vt = _qkv(xf, mod, i, norm_mix[i], wq, wk, wv, qg, kg, seq=seq, tm=TQ,
                             dilations=tuple(r for _, r in DIL_PATTERNS))
            y = _attention(qt, k, vt, dil_bias, _dilated_plan(seq, DIL_PATTERNS), seq=seq, diff=False)
            w_o = dil_w_o[slot]
        xf = _post_mlp(xf, y, mod, i, w_o.astype(BF16), norm_mlp[i], mlp_w1[i].astype(BF16),
                       mlp_w2[i].astype(BF16), seq=seq, tm=2 * TQ, ff_chunk=1024)
    return xf.reshape(bsz, seq, d)
```

```python
import functools
import math

import numpy as np
import jax
import jax.numpy as jnp
from jax import lax
from jax.experimental import pallas as pl
from jax.experimental.pallas import tpu as pltpu

F32 = jnp.float32
BF16 = jnp.bfloat16

RMS_EPS = 1e-6
NEG_INF = -1e30
NUM_BUCKETS = 32
REL_MAX_DISTANCE = 2048
HEAD_DIM = 64
N_MIXERS = 3
CONV_WIDTH = 4
LRU_C = 8.0
DIL_PATTERNS = ((128, 1), (512, 4), (2048, 16))
LOG2E = math.log2(math.e)
MASKED_BUCKET = NUM_BUCKETS
BUCKET_PAD = 64

LANES = 128
SUBLANES = 8
BF16_ROWS = 16
MXU_WIDTH = 256
VMEM_LIMIT_BYTES = 56 * 1024 * 1024

TQ = MXU_WIDTH
SCORE_LEAD = 5
ZERO_SHIFT = 20


def _params(*sem):
    return pltpu.CompilerParams(dimension_semantics=sem, vmem_limit_bytes=VMEM_LIMIT_BYTES)


def _resident(shape):
    nd = len(shape)
    return pl.BlockSpec(shape, lambda *_: (0,) * nd, pipeline_mode=pl.Buffered(1))


def _norm_mod(x, gain, shift, scale):
    ms = jnp.mean(x * x, axis=-1, keepdims=True)
    y = (x * lax.rsqrt(ms + RMS_EPS)) * gain
    return y * (1.0 + scale) + shift


def _ada_kernel(c_ref, w_ref, b_ref, o_ref):
    c = c_ref[...]
    a = (c * jax.nn.sigmoid(c)).astype(BF16)
    o_ref[0] = jnp.dot(a, w_ref[0].astype(BF16), preferred_element_type=F32) + b_ref[0]


def _ada(c, w_ada, b_ada):
    depth, d, n = w_ada.shape
    bsz = c.shape[0]
    tn = n // 4
    return pl.pallas_call(
        _ada_kernel,
        out_shape=jax.ShapeDtypeStruct((depth, bsz, n), F32),
        grid=(depth, n // tn),
        in_specs=[pl.BlockSpec((bsz, d), lambda l, j: (0, 0)),
                  pl.BlockSpec((1, d, tn), lambda l, j: (l, 0, j)),
                  pl.BlockSpec((1, 1, tn), lambda l, j: (l, 0, j))],
        out_specs=pl.BlockSpec((1, bsz, tn), lambda l, j: (l, 0, j)),
        compiler_params=_params("arbitrary", "arbitrary"),
        name="ada_mod",
    )(c, w_ada, b_ada.reshape(depth, 1, n))


def _t5_bucket_np(n):
    n = np.maximum(n, 0)
    max_exact = NUM_BUCKETS // 2
    nf = np.maximum(n, max_exact).astype(np.float32)
    large = max_exact + (np.log(nf / np.float32(max_exact)) / np.float32(math.log(REL_MAX_DISTANCE / max_exact))
                         * np.float32(NUM_BUCKETS - max_exact)).astype(np.int32)
    large = np.minimum(large, NUM_BUCKETS - 1)
    return np.where(n < max_exact, n, large)


def _bucket_tiles(nd, dilation, window, class_len):
    key = np.arange(TQ)[:, None]
    qry = np.arange(TQ)[None, :]
    tiles = []
    for delta in range(nd):
        steps = TQ * delta + qry - key
        valid = steps >= 0
        if window is not None:
            valid &= steps <= window
        if class_len < TQ:
            valid &= (key // class_len == qry // class_len) & (delta == 0)
        bucket = _t5_bucket_np(steps * dilation)
        tiles.append(np.where(valid, bucket, MASKED_BUCKET).reshape(1, TQ * TQ))
    return np.stack(tiles).astype(np.int32)


def _bias_kernel(bk_ref, rb_ref, o_ref):
    rb = rb_ref[...] * LOG2E
    hi = rb.astype(BF16)
    r1 = rb - hi.astype(F32)
    mid = r1.astype(BF16)
    low = (r1 - mid.astype(F32)).astype(BF16)
    n = o_ref.shape[2]
    chunk = 8192
    for c in range(n // chunk):
        bk = bk_ref[0, :, c * chunk:(c + 1) * chunk]
        onehot = lax.broadcasted_iota(jnp.int32, (BUCKET_PAD, chunk), 0) == bk
        oh = jnp.where(onehot, 1.0, 0.0).astype(BF16)
        o_ref[0, :, c * chunk:(c + 1) * chunk] = (
            jnp.dot(hi, oh, preferred_element_type=F32)
            + jnp.dot(mid, oh, preferred_element_type=F32)
            + jnp.dot(low, oh, preferred_element_type=F32))


def _bias_tiles(rb_ext, nd, dilation, window, class_len):
    buckets = jnp.asarray(_bucket_tiles(nd, dilation, window, class_len))
    ncols = rb_ext.shape[0]
    out = pl.pallas_call(
        _bias_kernel,
        out_shape=jax.ShapeDtypeStruct((nd, ncols, TQ * TQ), F32),
        grid=(nd,),
        in_specs=[pl.BlockSpec((1, 1, TQ * TQ), lambda i: (i, 0, 0)),
                  pl.BlockSpec((ncols, BUCKET_PAD), lambda i: (0, 0))],
        out_specs=pl.BlockSpec((1, ncols, TQ * TQ), lambda i: (i, 0, 0)),
        compiler_params=_params("arbitrary"),
        name="rel_bias_tiles",
    )(buckets, rb_ext)
    return out.reshape(nd, ncols, TQ, TQ)


def _class_major_rows(x_ref, u, tm, dilation):
    seq = x_ref.shape[0]
    class_len = seq // dilation
    if dilation == 1:
        return x_ref[pl.ds(pl.multiple_of(u * tm, tm), tm), :]
    if class_len >= tm:
        per_class = class_len // tm
        start = (u % per_class) * (tm * dilation) + u // per_class
        return x_ref[pl.ds(start, tm, stride=dilation), :]
    classes = tm // class_len
    return jnp.concatenate(
        [x_ref[pl.ds(u * classes + c, class_len, stride=dilation), :] for c in range(classes)], axis=0)


def _qkv_kernel(*refs, tm, dilations):
    nx = len(refs) - 11
    x_refs = refs[:nx]
    g_ref, sh_ref, sc_ref, wq_ref, wk_ref, wv_ref, qg_ref, kg_ref, qt_ref, k_ref, vt_ref = refs[nx:]
    groups = len(dilations)
    d = g_ref.shape[1]
    u = pl.program_id(0) % (x_refs[0].shape[0] // tm)
    nt = (((1,), (1,)), ((), ()))
    lo = lax.broadcasted_iota(jnp.int32, (tm, LANES), 1) < HEAD_DIM
    nsub = tm // TQ
    for g in range(groups):
        x = jnp.concatenate([_class_major_rows(xr, u, tm, dilations[g]) for xr in x_refs], axis=1)
        h = _norm_mod(x, g_ref[...], sh_ref[0], sc_ref[0]).astype(BF16)
        yq = lax.dot_general(wq_ref[g], h, nt, preferred_element_type=F32)
        y3 = yq.reshape(d // HEAD_DIM, HEAD_DIM, tm)
        ms = jnp.mean(y3 * y3, axis=1, keepdims=True)
        y3 = (y3 * lax.rsqrt(ms + RMS_EPS)) * qg_ref[...][None]
        yb = y3.reshape(d // LANES, LANES, tm).astype(BF16)
        for t in range(nsub):
            qt_ref[g, :, t] = yb[:, :, t * TQ:(t + 1) * TQ]
        yk = jnp.dot(h, wk_ref[g], preferred_element_type=F32)
        for c in range(d // LANES):
            yc = yk[:, c * LANES:(c + 1) * LANES]
            y2 = yc * yc
            s_lo = jnp.sum(jnp.where(lo, y2, 0.0), axis=-1, keepdims=True)
            s_hi = jnp.sum(jnp.where(lo, 0.0, y2), axis=-1, keepdims=True)
            ms = jnp.where(lo, s_lo, s_hi) * (1.0 / HEAD_DIM)
            k_ref[g, :, c * LANES:(c + 1) * LANES] = (
                (yc * lax.rsqrt(ms + RMS_EPS)) * kg_ref[...]).astype(BF16)
        yv = lax.dot_general(wv_ref[g], h, nt, preferred_element_type=F32)
        yvb = yv.reshape(d // LANES, LANES, tm).astype(BF16)
        for t in range(nsub):
            vt_ref[g, :, t] = yvb[:, :, t * TQ:(t + 1) * TQ]


def _qkv(x, mod, layer, gain, wq, wk, wv, qg, kg, *, seq, tm, dilations):
    t, d = x.shape
    groups = wq.shape[0]
    bsz = t // seq
    nblk = d // LANES
    row = lambda i: layer * bsz + (i * tm) // seq
    tshape = (groups, nblk, t // TQ, LANES, TQ)
    if max(dilations) > 1:
        x_specs = [pl.BlockSpec((seq, LANES), lambda i, j=j: ((i * tm) // seq, j)) for j in range(nblk)]
    else:
        x_specs = [pl.BlockSpec((tm, d), lambda i: (i, 0))]
    for r in dilations:
        assert (seq // r) % tm == 0 or (tm % (seq // r) == 0 and (seq // r) % SUBLANES == 0)
    return pl.pallas_call(
        functools.partial(_qkv_kernel, tm=tm, dilations=tuple(dilations)),
        out_shape=(jax.ShapeDtypeStruct(tshape, BF16),
                   jax.ShapeDtypeStruct((groups, t, d), BF16),
                   jax.ShapeDtypeStruct(tshape, BF16)),
        grid=(t // tm,),
        in_specs=x_specs + [
                  pl.BlockSpec((1, d), lambda i: (0, 0)),
                  pl.BlockSpec((1, 1, d), lambda i: (row(i), 0, 0)),
                  pl.BlockSpec((1, 1, d), lambda i: (row(i), 0, 1)),
                  _resident(wq.shape), _resident(wk.shape), _resident(wv.shape),
                  pl.BlockSpec((HEAD_DIM, 1), lambda i: (0, 0)),
                  pl.BlockSpec((1, LANES), lambda i: (0, 0))],
        out_specs=(pl.BlockSpec((groups, nblk, tm // TQ, LANES, TQ), lambda i: (0, 0, i, 0, 0)),
                   pl.BlockSpec((groups, tm, d), lambda i: (0, i, 0)),
                   pl.BlockSpec((groups, nblk, tm // TQ, LANES, TQ), lambda i: (0, 0, i, 0, 0))),
        compiler_params=_params("arbitrary"),
        name=f"qkv_proj_g{groups}",
    )(*([x] * len(x_specs)), gain.reshape(1, d), mod, mod, wq, wk, wv, qg, kg)


def _diff_plan(nq):
    return [(0, u, [(kj, u - kj) for kj in range(u + 1)], [(0, TQ, u * TQ, 1)]) for u in range(nq)]


def _dilated_plan(seq, patterns):
    plan = []
    for g, (window, r) in enumerate(patterns):
        class_len = seq // r
        assert window // r <= TQ
        for u in range(seq // TQ):
            if class_len >= TQ:
                per_class = class_len // TQ
                c, lt = u // per_class, u % per_class
                tiles = ([(u - 1, 1)] if lt > 0 else []) + [(u, 0)]
                stores = [(0, TQ, lt * TQ * r + c, r)]
            else:
                classes = TQ // class_len
                tiles = [(u, 0)]
                stores = [(h * class_len, class_len, u * classes + h, r) for h in range(classes)]
            plan.append((g, u, tiles, stores))
    return plan


def _attn_kernel(*refs, plan, nseg, diff, lambda_init):
    segs = [refs[4 * s:4 * s + 4] for s in range(nseg)]
    pos = 4 * nseg
    if diff:
        lamv_ref, subg_ref = refs[pos], refs[pos + 1]
        pos += 2
    o_ref, s_ref = refs[pos], refs[pos + 1]
    if not diff:
        og_ref, lse_ref = refs[pos + 2], refs[pos + 3]
    top = lax.broadcasted_iota(jnp.int32, (LANES, TQ), 0) < HEAD_DIM
    if diff:
        lv = lamv_ref[...]
        lam = (jnp.exp(jnp.sum(lv[0:1] * lv[1:2], axis=-1, keepdims=True))
               - jnp.exp(jnp.sum(lv[2:3] * lv[3:4], axis=-1, keepdims=True)) + lambda_init)

    ones_rows = jnp.ones((BF16_ROWS, TQ), BF16)
    dyn0 = lax.shift_right_logical(pl.program_id(1), ZERO_SHIFT)
    state = {}
    slot = {}

    def tiles_of(i):
        return plan[i][2]

    def rows(ref, g, start, n, stride):
        return ref.at[g, pl.ds(start, n, stride=stride) if stride > 1 else pl.ds(start, n), :]

    def scores(i, n):
        st_i = state.setdefault(i, {"m": [None, None], "mh": ([], []), "acc": [None, None], "q": None})
        g, u, tiles, _ = plan[i]
        kj, bidx = tiles[n]
        qt_ref, k_ref, _, b_ref = segs[g]
        if st_i["q"] is None:
            qt = qt_ref[0, 0, u]
            zero = jnp.zeros_like(qt)
            st_i["q"] = (jnp.where(top, qt, zero), jnp.where(top, zero, qt))
        kk = k_ref[0, kj * TQ:(kj + 1) * TQ, :]
        for st in range(2):
            sc = jnp.dot(kk, st_i["q"][st], preferred_element_type=F32) + b_ref[bidx, st]
            s_ref[slot[(i, n)] + dyn0, st] = sc
            cm = jnp.max(sc, axis=0, keepdims=True)
            st_i["m"][st] = cm if st_i["m"][st] is None else jnp.maximum(st_i["m"][st], cm)
            st_i["mh"][st].append(st_i["m"][st])

    def weigh(i, n):
        st_i = state[i]
        g, _, tiles, _ = plan[i]
        vt1 = jnp.concatenate([segs[g][2][0, 0, tiles[n][0]], ones_rows], axis=0)
        for st in range(2):
            mh = st_i["mh"][st]
            p = jnp.exp2(s_ref[slot[(i, n)] + dyn0, st] - mh[n])
            pv = jnp.dot(vt1, p.astype(BF16), preferred_element_type=F32)
            st_i["acc"][st] = pv if n == 0 else jnp.exp2(mh[n - 1] - mh[n]) * st_i["acc"][st] + pv

    def finish(i):
        g, _, _, stores = plan[i]
        st_i = state.pop(i)
        acc, m = st_i["acc"], st_i["m"]
        la, lb = acc[0][LANES:LANES + 1], acc[1][LANES:LANES + 1]
        oa = acc[0][:LANES] * (1.0 / la)
        ob = acc[1][:LANES] * (1.0 / lb)
        if diff:
            ot = oa - lam * ob
            ms = jnp.mean(ot * ot, axis=0, keepdims=True)
            ot = ((ot * lax.rsqrt(ms + RMS_EPS)) * subg_ref[...]) * (1.0 - lambda_init)
            for row0, n, dst, stride in stores:
                assert stride == 1
                o_ref[dst:dst + n, :] = ot.T[row0:row0 + n].astype(BF16)
        else:
            o_rows = jnp.where(top, oa, ob).T
            lse_rows = jnp.where(top, m[0] + jnp.log2(la), m[1] + jnp.log2(lb)).T
            for row0, n, dst, stride in stores:
                rows(og_ref, g, dst, n, stride)[...] = o_rows[row0:row0 + n]
                rows(lse_ref, g, dst, n, stride)[...] = lse_rows[row0:row0 + n]

    def chunks_of(i):
        return {(dst + t * stride) // TQ for _, n, dst, stride in plan[i][3] for t in range(n)}

    nchunk = o_ref.shape[0] // TQ
    if diff:
        by_size = sorted(range(len(plan)), key=lambda i: len(tiles_of(i)))
        order = by_size[0::2] + by_size[1::2][::-1]
    else:
        order = []
        for c in range(nchunk):
            order += [i for i in range(len(plan)) if c in chunks_of(i) and i not in order]
    needs = {c: {i for i in range(len(plan)) if c in chunks_of(i)} for c in range(nchunk)}
    finished, merged = set(), set()

    def merge_ready():
        for c in range(nchunk):
            if c in merged or not needs[c] <= finished:
                continue
            merged.add(c)
            r0 = c * TQ
            lses = [lse_ref[g, r0:r0 + TQ, :] for g in range(nseg)]
            top_lse = functools.reduce(jnp.maximum, lses)
            ws = [jnp.exp2(l - top_lse) for l in lses]
            num = sum(w * og_ref[g, r0:r0 + TQ, :] for g, w in enumerate(ws))
            o_ref[r0:r0 + TQ, :] = (num * (1.0 / sum(ws))).astype(BF16)

    flat = [(i, n) for i in order for n in range(len(tiles_of(i)))]
    assert s_ref.shape[0] > SCORE_LEAD
    for t, (i, n) in enumerate(flat):
        slot[(i, n)] = t % s_ref.shape[0]
    issued = 0
    for t, (i, n) in enumerate(flat):
        while issued <= min(t + SCORE_LEAD, len(flat) - 1):
            scores(*flat[issued])
            issued += 1
        weigh(i, n)
        if n == len(tiles_of(i)) - 1:
            finish(i)
            finished.add(i)
            if not diff:
                merge_ready()
    assert diff or len(merged) == nchunk


def _attention(qt, k, vt, biases, plan, *, seq, diff, lambda_init=0.0, lamv=None, subg=None):
    groups, nblk, ntile, _, _ = qt.shape
    t, d = k.shape[1], k.shape[2]
    bsz = t // seq
    nq = seq // TQ
    nds = tuple(b.shape[0] for b in biases)
    assert bsz < 2 ** ZERO_SHIFT
    scratch = [pltpu.VMEM((SCORE_LEAD + 2, 2, TQ, TQ), F32)]
    if not diff:
        scratch += [pltpu.VMEM((groups, seq, LANES), F32), pltpu.VMEM((groups, seq, LANES), F32)]
    in_specs, args = [], []
    for g in range(groups):
        in_specs += [
            pl.BlockSpec((1, 1, nq, LANES, TQ), lambda j, b, g=g: (g, j, b, 0, 0)),
            pl.BlockSpec((1, seq, LANES), lambda j, b, g=g: (g, b, j)),
            pl.BlockSpec((1, 1, nq, LANES, TQ), lambda j, b, g=g: (g, j, b, 0, 0)),
            pl.BlockSpec((nds[g], 2, TQ, TQ), lambda j, b: (0, j, 0, 0)),
        ]
        args += [qt, k, vt, biases[g]]
    if diff:
        in_specs += [pl.BlockSpec(lamv.shape, lambda j, b: (0, 0)),
                     pl.BlockSpec((LANES, 1), lambda j, b: (0, 0))]
        args += [lamv, subg]
    return pl.pallas_call(
        functools.partial(_attn_kernel, plan=plan, nseg=groups, diff=diff, lambda_init=lambda_init),
        out_shape=jax.ShapeDtypeStruct((t, d), BF16),
        grid=(nblk, bsz),
        in_specs=in_specs,
        out_specs=pl.BlockSpec((seq, LANES), lambda j, b: (b, j)),
        scratch_shapes=scratch,
        compiler_params=_params("arbitrary", "arbitrary"),
        name="diff_attention" if diff else "dilated_attention",
    )(*args)


def _lru_kernel(x_ref, g_ref, sh_ref, sc_ref, win_ref, cw_ref, cb_ref, wx_ref, bx_ref,
                wa_ref, ba_ref, ap_ref, o_ref, uext_ref, h_ref):
    ts, d = x_ref.shape

    @pl.when(pl.program_id(1) == 0)
    def _():
        uext_ref[0:SUBLANES, :] = jnp.zeros((SUBLANES, d), F32)
        h_ref[...] = jnp.zeros_like(h_ref)

    h = _norm_mod(x_ref[...], g_ref[...], sh_ref[0], sc_ref[0]).astype(BF16)
    yu = jnp.dot(h, win_ref[...], preferred_element_type=F32)
    y = yu[:, :d]
    u = yu[:, d:]
    uext_ref[SUBLANES:SUBLANES + ts, :] = u
    conv = cb_ref[...] + u * cw_ref[0:1, :]
    for tap in range(1, CONV_WIDTH):
        conv = conv + uext_ref[SUBLANES - tap:SUBLANES - tap + ts, :] * cw_ref[tap:tap + 1, :]
    uext_ref[0:SUBLANES, :] = u[ts - SUBLANES:ts, :]
    uc = conv
    ucb = uc.astype(BF16)
    nblk = wx_ref.shape[0]
    bw = d // nblk
    gx = jnp.concatenate([jnp.dot(ucb[:, n * bw:(n + 1) * bw], wx_ref[n], preferred_element_type=F32)
                          for n in range(nblk)], axis=1) + bx_ref[...]
    ga = jnp.concatenate([jnp.dot(ucb[:, n * bw:(n + 1) * bw], wa_ref[n], preferred_element_type=F32)
                          for n in range(nblk)], axis=1) + ba_ref[...]
    gate_x = jax.nn.sigmoid(gx)
    gate_a = jax.nn.sigmoid(ga)
    z = -ap_ref[...]
    softplus = jnp.maximum(z, 0.0) + jnp.log1p(jnp.exp(-jnp.abs(z)))
    log_a = (-LRU_C * gate_a) * softplus
    a = jnp.exp(log_a)
    b = jnp.sqrt(-jnp.tanh(log_a) * (1.0 + a * a)) * (gate_x * uc)
    ngrp = ts // SUBLANES
    a = a.reshape(ngrp, SUBLANES, d)
    b = b.reshape(ngrp, SUBLANES, d)
    sub = lax.broadcasted_iota(jnp.int32, (ngrp, SUBLANES, d), 1)
    step = 1
    while step < SUBLANES:
        keep = sub >= step
        a_prev = pltpu.roll(a, step, 1)
        b_prev = pltpu.roll(b, step, 1)
        b = jnp.where(keep, a * b_prev, 0.0) + b
        a = jnp.where(keep, a * a_prev, a)
        step *= 2
    carry = h_ref[...]
    groups = []
    for r in range(ngrp):
        hg = a[r] * carry + b[r]
        groups.append(hg)
        carry = hg[SUBLANES - 1:SUBLANES, :]
    hs = jnp.concatenate(groups, axis=0)
    h_ref[...] = carry
    gelu = 0.5 * y * (1.0 + jnp.tanh(math.sqrt(2.0 / math.pi) * (y + 0.044715 * (y * y * y))))
    o_ref[...] = (hs * gelu).astype(BF16)


def _lru(x, mod, layer, gain, w_in, conv_w, conv_b, w_x, b_x, w_a, b_a, a_param, *, seq, ts):
    t, d = x.shape
    bsz = t // seq
    ns = seq // ts
    vec = lambda v: v.reshape(1, d)
    return pl.pallas_call(
        _lru_kernel,
        out_shape=jax.ShapeDtypeStruct((t, d), BF16),
        grid=(bsz, ns),
        in_specs=[pl.BlockSpec((ts, d), lambda b, s: (b * ns + s, 0)),
                  pl.BlockSpec((1, d), lambda b, s: (0, 0)),
                  pl.BlockSpec((1, 1, d), lambda b, s: (layer * bsz + b, 0, 0)),
                  pl.BlockSpec((1, 1, d), lambda b, s: (layer * bsz + b, 0, 1)),
                  _resident(w_in.shape),
                  pl.BlockSpec((CONV_WIDTH, d), lambda b, s: (0, 0)),
                  pl.BlockSpec((1, d), lambda b, s: (0, 0)),
                  _resident(w_x.shape),
                  pl.BlockSpec((1, d), lambda b, s: (0, 0)),
                  _resident(w_a.shape),
                  pl.BlockSpec((1, d), lambda b, s: (0, 0)),
                  pl.BlockSpec((1, d), lambda b, s: (0, 0))],
        out_specs=pl.BlockSpec((ts, d), lambda b, s: (b * ns + s, 0)),
        scratch_shapes=[pltpu.VMEM((ts + SUBLANES, d), F32), pltpu.VMEM((1, d), F32)],
        compiler_params=_params("arbitrary", "arbitrary"),
        name="rglru_mixer",
    )(x, vec(gain), mod, mod, w_in, conv_w, vec(conv_b), w_x, vec(b_x), w_a, vec(b_a), vec(a_param))


def _post_mlp_kernel(x_ref, y_ref, wo_ref, g1_ref, g_ref, sh_ref, sc_ref, g2_ref, w1_ref, w2_ref,
                     o_ref, *, ff_chunk):
    x1 = x_ref[...] + g1_ref[0] * jnp.dot(y_ref[...], wo_ref[...], preferred_element_type=F32)
    h = _norm_mod(x1, g_ref[...], sh_ref[0], sc_ref[0]).astype(BF16)
    dff = w1_ref.shape[1]
    acc = None
    for c in range(dff // ff_chunk):
        a = jnp.dot(h, w1_ref[:, c * ff_chunk:(c + 1) * ff_chunk], preferred_element_type=F32)
        a = jnp.maximum(a, 0.0)
        a = (a * a).astype(BF16)
        part = jnp.dot(a, w2_ref[c * ff_chunk:(c + 1) * ff_chunk, :], preferred_element_type=F32)
        acc = part if acc is None else acc + part
    o_ref[...] = x1 + g2_ref[0] * acc


def _post_mlp(x, y, mod, layer, w_o, gain, w1, w2, *, seq, tm, ff_chunk):
    t, d = x.shape
    bsz = t // seq
    row = lambda i: layer * bsz + (i * tm) // seq
    part = lambda p: pl.BlockSpec((1, 1, d), lambda i: (row(i), 0, p))
    return pl.pallas_call(
        functools.partial(_post_mlp_kernel, ff_chunk=ff_chunk),
        out_shape=jax.ShapeDtypeStruct((t, d), F32),
        grid=(t // tm,),
        in_specs=[pl.BlockSpec((tm, d), lambda i: (i, 0)),
                  pl.BlockSpec((tm, d), lambda i: (i, 0)),
                  _resident(w_o.shape),
                  part(2),
                  pl.BlockSpec((1, d), lambda i: (0, 0)),
                  part(3), part(4), part(5),
                  _resident(w1.shape), _resident(w2.shape)],
        out_specs=pl.BlockSpec((tm, d), lambda i: (i, 0)),
        compiler_params=_params("arbitrary"),
        name="outproj_mlp",
    )(x, y, w_o, mod, gain.reshape(1, d), mod, mod, mod, w1, w2)


def kernel(x, c, rel_bias, w_ada, b_ada, norm_mix, norm_mlp, mlp_w1, mlp_w2, da_w_qkv, da_w_o, da_q_gain, da_k_gain, da_lam_q1, da_lam_k1, da_lam_q2, da_lam_k2, da_sub_gain, lru_w_in, lru_conv_w, lru_conv_b, lru_w_x, lru_b_x, lru_w_a, lru_b_a, lru_a_param, lru_w_out, dil_w_qkv, dil_w_o, dil_q_gain, dil_k_gain):
    bsz, seq, d = x.shape
    depth = w_ada.shape[0]
    t = bsz * seq
    nq = seq // TQ
    assert seq % TQ == 0 and d % LANES == 0

    mod = _ada(c, w_ada, b_ada).reshape(depth * bsz, 1, 6 * d)

    ncols = rel_bias.shape[1]
    rb_ext = jnp.concatenate(
        [rel_bias.T.astype(F32), jnp.full((ncols, 1), NEG_INF, F32),
         jnp.zeros((ncols, BUCKET_PAD - NUM_BUCKETS - 1), F32)], axis=1)
    kinds = [i % N_MIXERS for i in range(depth)]
    da_bias = _bias_tiles(rb_ext, nq, 1, None, seq) if 0 in kinds else None
    dil_bias = None
    if 2 in kinds:
        dil_bias = [_bias_tiles(rb_ext, 2 if seq // r >= 2 * TQ else 1, r, w // r, seq // r)
                    for (w, r) in DIL_PATTERNS]

    q_scale = HEAD_DIM ** -0.5 * LOG2E
    xf = x.reshape(t, d)
    for i in range(depth):
        kind, slot = kinds[i], i // N_MIXERS
        if kind == 0:
            w = da_w_qkv[slot]
            wq = w[:, :d].T.astype(BF16)[None]
            wk = w[:, d:2 * d].astype(BF16)[None]
            wv = w[:, 2 * d:].T.astype(BF16)[None]
            qg = (da_q_gain[slot] * q_scale).reshape(HEAD_DIM, 1)
            kg = jnp.tile(da_k_gain[slot], 2).reshape(1, LANES)
            qt, k, vt = _qkv(xf, mod, i, norm_mix[i], wq, wk, wv, qg, kg, seq=seq, tm=2 * TQ,
                             dilations=(1,))
            lambda_init = 0.8 - 0.6 * math.exp(-0.3 * i)
            lamv = jnp.stack([da_lam_q1[slot], da_lam_k1[slot], da_lam_q2[slot], da_lam_k2[slot]]).astype(F32)
            y = _attention(qt, k, vt, [da_bias], _diff_plan(nq), seq=seq, diff=True, lambda_init=lambda_init,
                           lamv=lamv, subg=da_sub_gain[slot].reshape(LANES, 1))
            w_o = da_w_o[slot]
        elif kind == 1:
            y = _lru(xf, mod, i, norm_mix[i], lru_w_in[slot].astype(BF16), lru_conv_w[slot],
                     lru_conv_b[slot], lru_w_x[slot].astype(BF16), lru_b_x[slot],
                     lru_w_a[slot].astype(BF16), lru_b_a[slot], lru_a_param[slot], seq=seq, ts=TQ)
            w_o = lru_w_out[slot]
        else:
            w = dil_w_qkv[slot].reshape(d, len(DIL_PATTERNS), 3, d)
            wq = jnp.transpose(w[:, :, 0, :], (1, 2, 0)).astype(BF16)
            wk = jnp.transpose(w[:, :, 1, :], (1, 0, 2)).astype(BF16)
            wv = jnp.transpose(w[:, :, 2, :], (1, 2, 0)).astype(BF16)
            qg = (dil_q_gain[slot] * q_scale).reshape(HEAD_DIM, 1)
            kg = jnp.tile(dil_k_gain[slot], 2).reshape(1, LANES)
            qt, k, vt = _qkv(xf, mod, i, norm_mix[i], wq, wk, wv, qg, kg, seq=seq, tm=TQ,
                             dilations=tuple(r for _, r in DIL_PATTERNS))
            y = _attention(qt, k, vt, dil_bias, _dilated_plan(seq, DIL_PATTERNS), seq=seq, diff=False)
            w_o = dil_w_o[slot]
        xf = _post_mlp(xf, y, mod, i, w_o.astype(BF16), norm_mlp[i], mlp_w1[i].astype(BF16),
                       mlp_w2[i].astype(BF16), seq=seq, tm=2 * TQ, ff_chunk=1024)
    return xf.reshape(bsz, seq, d)
```

```python
import functools
import math

import numpy as np
import jax
import jax.numpy as jnp
from jax import lax
from jax.experimental import pallas as pl
from jax.experimental.pallas import tpu as pltpu

F32 = jnp.float32
BF16 = jnp.bfloat16

RMS_EPS = 1e-6
NEG_INF = -1e30
NUM_BUCKETS = 32
REL_MAX_DISTANCE = 2048
HEAD_DIM = 64
N_MIXERS = 3
CONV_WIDTH = 4
LRU_C = 8.0
DIL_PATTERNS = ((128, 1), (512, 4), (2048, 16))
LOG2E = math.log2(math.e)
MASKED_BUCKET = NUM_BUCKETS
BUCKET_PAD = 64

LANES = 128
SUBLANES = 8
BF16_ROWS = 16
MXU_WIDTH = 256
VMEM_LIMIT_BYTES = 56 * 1024 * 1024

TQ = MXU_WIDTH
SCORE_LEAD = 5
ZERO_SHIFT = 20
BIG_TILE_ROWS = 4 * MXU_WIDTH
MLP_FF_CHUNK = 2 * MXU_WIDTH


def _params(*sem):
    return pltpu.CompilerParams(dimension_semantics=sem, vmem_limit_bytes=VMEM_LIMIT_BYTES)


def _resident(shape):
    nd = len(shape)
    return pl.BlockSpec(shape, lambda *_: (0,) * nd, pipeline_mode=pl.Buffered(1))


def _norm_mod(x, gain, shift, scale):
    ms = jnp.mean(x * x, axis=-1, keepdims=True)
    y = (x * lax.rsqrt(ms + RMS_EPS)) * gain
    return y * (1.0 + scale) + shift


def _ada_kernel(c_ref, w_ref, b_ref, o_ref):
    c = c_ref[...]
    a = (c * jax.nn.sigmoid(c)).astype(BF16)
    o_ref[0] = jnp.dot(a, w_ref[0].astype(BF16), preferred_element_type=F32) + b_ref[0]


def _ada(c, w_ada, b_ada):
    depth, d, n = w_ada.shape
    bsz = c.shape[0]
    tn = n // 4
    return pl.pallas_call(
        _ada_kernel,
        out_shape=jax.ShapeDtypeStruct((depth, bsz, n), F32),
        grid=(depth, n // tn),
        in_specs=[pl.BlockSpec((bsz, d), lambda l, j: (0, 0)),
                  pl.BlockSpec((1, d, tn), lambda l, j: (l, 0, j)),
                  pl.BlockSpec((1, 1, tn), lambda l, j: (l, 0, j))],
        out_specs=pl.BlockSpec((1, bsz, tn), lambda l, j: (l, 0, j)),
        compiler_params=_params("arbitrary", "arbitrary"),
        name="ada_mod",
    )(c, w_ada, b_ada.reshape(depth, 1, n))


def _t5_bucket_np(n):
    n = np.maximum(n, 0)
    max_exact = NUM_BUCKETS // 2
    nf = np.maximum(n, max_exact).astype(np.float32)
    large = max_exact + (np.log(nf / np.float32(max_exact)) / np.float32(math.log(REL_MAX_DISTANCE / max_exact))
                         * np.float32(NUM_BUCKETS - max_exact)).astype(np.int32)
    large = np.minimum(large, NUM_BUCKETS - 1)
    return np.where(n < max_exact, n, large)


def _bucket_tiles(nd, dilation, window, class_len):
    key = np.arange(TQ)[:, None]
    qry = np.arange(TQ)[None, :]
    tiles = []
    for delta in range(nd):
        steps = TQ * delta + qry - key
        valid = steps >= 0
        if window is not None:
            valid &= steps <= window
        if class_len < TQ:
            valid &= (key // class_len == qry // class_len) & (delta == 0)
        bucket = _t5_bucket_np(steps * dilation)
        tiles.append(np.where(valid, bucket, MASKED_BUCKET).reshape(1, TQ * TQ))
    return np.stack(tiles).astype(np.int32)


def _bias_kernel(bk_ref, rb_ref, o_ref):
    rb = rb_ref[...] * LOG2E
    hi = rb.astype(BF16)
    r1 = rb - hi.astype(F32)
    mid = r1.astype(BF16)
    low = (r1 - mid.astype(F32)).astype(BF16)
    n = o_ref.shape[2]
    chunk = 8192
    for c in range(n // chunk):
        bk = bk_ref[0, :, c * chunk:(c + 1) * chunk]
        onehot = lax.broadcasted_iota(jnp.int32, (BUCKET_PAD, chunk), 0) == bk
        oh = jnp.where(onehot, 1.0, 0.0).astype(BF16)
        o_ref[0, :, c * chunk:(c + 1) * chunk] = (
            jnp.dot(hi, oh, preferred_element_type=F32)
            + jnp.dot(mid, oh, preferred_element_type=F32)
            + jnp.dot(low, oh, preferred_element_type=F32))


def _bias_tiles(rb_ext, nd, dilation, window, class_len):
    buckets = jnp.asarray(_bucket_tiles(nd, dilation, window, class_len))
    ncols = rb_ext.shape[0]
    out = pl.pallas_call(
        _bias_kernel,
        out_shape=jax.ShapeDtypeStruct((nd, ncols, TQ * TQ), F32),
        grid=(nd,),
        in_specs=[pl.BlockSpec((1, 1, TQ * TQ), lambda i: (i, 0, 0)),
                  pl.BlockSpec((ncols, BUCKET_PAD), lambda i: (0, 0))],
        out_specs=pl.BlockSpec((1, ncols, TQ * TQ), lambda i: (i, 0, 0)),
        compiler_params=_params("arbitrary"),
        name="rel_bias_tiles",
    )(buckets, rb_ext)
    return out.reshape(nd, ncols, TQ, TQ)


def _class_major_rows(x_ref, u, tm, dilation):
    seq = x_ref.shape[0]
    class_len = seq // dilation
    if dilation == 1:
        return x_ref[pl.ds(pl.multiple_of(u * tm, tm), tm), :]
    if class_len >= tm:
        per_class = class_len // tm
        start = (u % per_class) * (tm * dilation) + u // per_class
        return x_ref[pl.ds(start, tm, stride=dilation), :]
    classes = tm // class_len
    return jnp.concatenate(
        [x_ref[pl.ds(u * classes + c, class_len, stride=dilation), :] for c in range(classes)], axis=0)


def _qkv_kernel(*refs, tm, dilations):
    nx = len(refs) - 11
    x_refs = refs[:nx]
    g_ref, sh_ref, sc_ref, wq_ref, wk_ref, wv_ref, qg_ref, kg_ref, qt_ref, k_ref, vt_ref = refs[nx:]
    groups = len(dilations)
    d = g_ref.shape[1]
    u = pl.program_id(0) % (x_refs[0].shape[0] // tm)
    nt = (((1,), (1,)), ((), ()))
    lo = lax.broadcasted_iota(jnp.int32, (tm, LANES), 1) < HEAD_DIM
    nsub = tm // TQ
    for g in range(groups):
        x = jnp.concatenate([_class_major_rows(xr, u, tm, dilations[g]) for xr in x_refs], axis=1)
        h = _norm_mod(x, g_ref[...], sh_ref[0], sc_ref[0]).astype(BF16)
        yk = jnp.dot(h, wk_ref[g], preferred_element_type=F32)
        for c in range(d // LANES):
            yc = yk[:, c * LANES:(c + 1) * LANES]
            y2 = yc * yc
            s_lo = jnp.sum(jnp.where(lo, y2, 0.0), axis=-1, keepdims=True)
            s_hi = jnp.sum(jnp.where(lo, 0.0, y2), axis=-1, keepdims=True)
            ms = jnp.where(lo, s_lo, s_hi) * (1.0 / HEAD_DIM)
            k_ref[g, :, c * LANES:(c + 1) * LANES] = (
                (yc * lax.rsqrt(ms + RMS_EPS)) * kg_ref[...]).astype(BF16)
        yq = lax.dot_general(wq_ref[g], h, nt, preferred_element_type=F32)
        y3 = yq.reshape(d // HEAD_DIM, HEAD_DIM, tm)
        ms = jnp.mean(y3 * y3, axis=1, keepdims=True)
        y3 = (y3 * lax.rsqrt(ms + RMS_EPS)) * qg_ref[...][None]
        yb = y3.reshape(d // LANES, LANES, tm).astype(BF16)
        for t in range(nsub):
            qt_ref[g, :, t] = yb[:, :, t * TQ:(t + 1) * TQ]
        yv = lax.dot_general(wv_ref[g], h, nt, preferred_element_type=F32)
        yvb = yv.reshape(d // LANES, LANES, tm).astype(BF16)
        for t in range(nsub):
            vt_ref[g, :, t] = yvb[:, :, t * TQ:(t + 1) * TQ]


def _qkv(x, mod, layer, gain, wq, wk, wv, qg, kg, *, seq, tm, dilations):
    t, d = x.shape
    groups = wq.shape[0]
    bsz = t // seq
    nblk = d // LANES
    row = lambda i: layer * bsz + (i * tm) // seq
    tshape = (groups, nblk, t // TQ, LANES, TQ)
    if max(dilations) > 1:
        x_specs = [pl.BlockSpec((seq, LANES), lambda i, j=j: ((i * tm) // seq, j)) for j in range(nblk)]
    else:
        x_specs = [pl.BlockSpec((tm, d), lambda i: (i, 0))]
    for r in dilations:
        assert (seq // r) % tm == 0 or (tm % (seq // r) == 0 and (seq // r) % SUBLANES == 0)
    return pl.pallas_call(
        functools.partial(_qkv_kernel, tm=tm, dilations=tuple(dilations)),
        out_shape=(jax.ShapeDtypeStruct(tshape, BF16),
                   jax.ShapeDtypeStruct((groups, t, d), BF16),
                   jax.ShapeDtypeStruct(tshape, BF16)),
        grid=(t // tm,),
        in_specs=x_specs + [
                  pl.BlockSpec((1, d), lambda i: (0, 0)),
                  pl.BlockSpec((1, 1, d), lambda i: (row(i), 0, 0)),
                  pl.BlockSpec((1, 1, d), lambda i: (row(i), 0, 1)),
                  _resident(wq.shape), _resident(wk.shape), _resident(wv.shape),
                  pl.BlockSpec((HEAD_DIM, 1), lambda i: (0, 0)),
                  pl.BlockSpec((1, LANES), lambda i: (0, 0))],
        out_specs=(pl.BlockSpec((groups, nblk, tm // TQ, LANES, TQ), lambda i: (0, 0, i, 0, 0)),
                   pl.BlockSpec((groups, tm, d), lambda i: (0, i, 0)),
                   pl.BlockSpec((groups, nblk, tm // TQ, LANES, TQ), lambda i: (0, 0, i, 0, 0))),
        compiler_params=_params("arbitrary"),
        name=f"qkv_proj_g{groups}",
    )(*([x] * len(x_specs)), gain.reshape(1, d), mod, mod, wq, wk, wv, qg, kg)


def _diff_plan(nq):
    return [(0, u, [(kj, u - kj) for kj in range(u + 1)], [(0, TQ, u * TQ, 1)]) for u in range(nq)]


def _dilated_plan(seq, patterns):
    plan = []
    for g, (window, r) in enumerate(patterns):
        class_len = seq // r
        assert window // r <= TQ
        for u in range(seq // TQ):
            if class_len >= TQ:
                per_class = class_len // TQ
                c, lt = u // per_class, u % per_class
                tiles = ([(u - 1, 1)] if lt > 0 else []) + [(u, 0)]
                stores = [(0, TQ, lt * TQ * r + c, r)]
            else:
                classes = TQ // class_len
                tiles = [(u, 0)]
                stores = [(h * class_len, class_len, u * classes + h, r) for h in range(classes)]
            plan.append((g, u, tiles, stores))
    return plan


def _attn_kernel(*refs, plan, nseg, diff, lambda_init):
    segs = [refs[4 * s:4 * s + 4] for s in range(nseg)]
    pos = 4 * nseg
    if diff:
        lamv_ref, subg_ref = refs[pos], refs[pos + 1]
        pos += 2
    o_ref, s_ref = refs[pos], refs[pos + 1]
    if not diff:
        og_ref, lse_ref = refs[pos + 2], refs[pos + 3]
    top = lax.broadcasted_iota(jnp.int32, (LANES, TQ), 0) < HEAD_DIM
    if diff:
        lv = lamv_ref[...]
        lam = (jnp.exp(jnp.sum(lv[0:1] * lv[1:2], axis=-1, keepdims=True))
               - jnp.exp(jnp.sum(lv[2:3] * lv[3:4], axis=-1, keepdims=True)) + lambda_init)

    ones_rows = jnp.ones((BF16_ROWS, TQ), BF16)
    dyn0 = lax.shift_right_logical(pl.program_id(1), ZERO_SHIFT)
    state = {}
    slot = {}

    def tiles_of(i):
        return plan[i][2]

    def rows(ref, g, start, n, stride):
        return ref.at[g, pl.ds(start, n, stride=stride) if stride > 1 else pl.ds(start, n), :]

    def scores(i, n):
        st_i = state.setdefault(i, {"m": [None, None], "mh": ([], []), "acc": [None, None], "q": None})
        g, u, tiles, _ = plan[i]
        kj, bidx = tiles[n]
        qt_ref, k_ref, _, b_ref = segs[g]
        if st_i["q"] is None:
            qt = qt_ref[0, 0, u]
            zero = jnp.zeros_like(qt)
            st_i["q"] = (jnp.where(top, qt, zero), jnp.where(top, zero, qt))
        kk = k_ref[0, kj * TQ:(kj + 1) * TQ, :]
        for st in range(2):
            sc = jnp.dot(kk, st_i["q"][st], preferred_element_type=F32) + b_ref[bidx, st]
            s_ref[slot[(i, n)] + dyn0, st] = sc
            cm = jnp.max(sc, axis=0, keepdims=True)
            st_i["m"][st] = cm if st_i["m"][st] is None else jnp.maximum(st_i["m"][st], cm)
            st_i["mh"][st].append(st_i["m"][st])

    def weigh(i, n):
        st_i = state[i]
        g, _, tiles, _ = plan[i]
        vt1 = jnp.concatenate([segs[g][2][0, 0, tiles[n][0]], ones_rows], axis=0)
        for st in range(2):
            mh = st_i["mh"][st]
            p = jnp.exp2(s_ref[slot[(i, n)] + dyn0, st] - mh[n])
            pv = jnp.dot(vt1, p.astype(BF16), preferred_element_type=F32)
            st_i["acc"][st] = pv if n == 0 else jnp.exp2(mh[n - 1] - mh[n]) * st_i["acc"][st] + pv

    def finish(i):
        g, _, _, stores = plan[i]
        st_i = state.pop(i)
        acc, m = st_i["acc"], st_i["m"]
        la, lb = acc[0][LANES:LANES + 1], acc[1][LANES:LANES + 1]
        oa = acc[0][:LANES] * (1.0 / la)
        ob = acc[1][:LANES] * (1.0 / lb)
        if diff:
            ot = oa - lam * ob
            ms = jnp.mean(ot * ot, axis=0, keepdims=True)
            ot = ((ot * lax.rsqrt(ms + RMS_EPS)) * subg_ref[...]) * (1.0 - lambda_init)
            for row0, n, dst, stride in stores:
                assert stride == 1
                o_ref[dst:dst + n, :] = ot.T[row0:row0 + n].astype(BF16)
        else:
            o_rows = jnp.where(top, oa, ob).T
            lse_rows = jnp.where(top, m[0] + jnp.log2(la), m[1] + jnp.log2(lb)).T
            for row0, n, dst, stride in stores:
                rows(og_ref, g, dst, n, stride)[...] = o_rows[row0:row0 + n]
                rows(lse_ref, g, dst, n, stride)[...] = lse_rows[row0:row0 + n]

    def chunks_of(i):
        return {(dst + t * stride) // TQ for _, n, dst, stride in plan[i][3] for t in range(n)}

    nchunk = o_ref.shape[0] // TQ
    if diff:
        by_size = sorted(range(len(plan)), key=lambda i: len(tiles_of(i)))
        order = by_size[0::2] + by_size[1::2][::-1]
    else:
        order = []
        for c in range(nchunk):
            order += [i for i in range(len(plan)) if c in chunks_of(i) and i not in order]
    needs = {c: {i for i in range(len(plan)) if c in chunks_of(i)} for c in range(nchunk)}
    finished, merged = set(), set()

    def merge_ready():
        for c in range(nchunk):
            if c in merged or not needs[c] <= finished:
                continue
            merged.add(c)
            r0 = c * TQ
            lses = [lse_ref[g, r0:r0 + TQ, :] for g in range(nseg)]
            top_lse = functools.reduce(jnp.maximum, lses)
            ws = [jnp.exp2(l - top_lse) for l in lses]
            num = sum(w * og_ref[g, r0:r0 + TQ, :] for g, w in enumerate(ws))
            o_ref[r0:r0 + TQ, :] = (num * (1.0 / sum(ws))).astype(BF16)

    flat = [(i, n) for i in order for n in range(len(tiles_of(i)))]
    assert s_ref.shape[0] > SCORE_LEAD
    for t, (i, n) in enumerate(flat):
        slot[(i, n)] = t % s_ref.shape[0]
    issued = 0
    for t, (i, n) in enumerate(flat):
        while issued <= min(t + SCORE_LEAD, len(flat) - 1):
            scores(*flat[issued])
            issued += 1
        weigh(i, n)
        if n == len(tiles_of(i)) - 1:
            finish(i)
            finished.add(i)
            if not diff:
                merge_ready()
    assert diff or len(merged) == nchunk


def _attention(qt, k, vt, biases, plan, *, seq, diff, lambda_init=0.0, lamv=None, subg=None):
    groups, nblk, ntile, _, _ = qt.shape
    t, d = k.shape[1], k.shape[2]
    bsz = t // seq
    nq = seq // TQ
    nds = tuple(b.shape[0] for b in biases)
    assert bsz < 2 ** ZERO_SHIFT
    scratch = [pltpu.VMEM((SCORE_LEAD + 2, 2, TQ, TQ), F32)]
    if not diff:
        scratch += [pltpu.VMEM((groups, seq, LANES), F32), pltpu.VMEM((groups, seq, LANES), F32)]
    in_specs, args = [], []
    for g in range(groups):
        in_specs += [
            pl.BlockSpec((1, 1, nq, LANES, TQ), lambda j, b, g=g: (g, j, b, 0, 0)),
            pl.BlockSpec((1, seq, LANES), lambda j, b, g=g: (g, b, j)),
            pl.BlockSpec((1, 1, nq, LANES, TQ), lambda j, b, g=g: (g, j, b, 0, 0)),
            pl.BlockSpec((nds[g], 2, TQ, TQ), lambda j, b: (0, j, 0, 0)),
        ]
        args += [qt, k, vt, biases[g]]
    if diff:
        in_specs += [pl.BlockSpec(lamv.shape, lambda j, b: (0, 0)),
                     pl.BlockSpec((LANES, 1), lambda j, b: (0, 0))]
        args += [lamv, subg]
    return pl.pallas_call(
        functools.partial(_attn_kernel, plan=plan, nseg=groups, diff=diff, lambda_init=lambda_init),
        out_shape=jax.ShapeDtypeStruct((t, d), BF16),
        grid=(nblk, bsz),
        in_specs=in_specs,
        out_specs=pl.BlockSpec((seq, LANES), lambda j, b: (b, j)),
        scratch_shapes=scratch,
        compiler_params=_params("arbitrary", "arbitrary"),
        name="diff_attention" if diff else "dilated_attention",
    )(*args)


def _lru_kernel(x_ref, g_ref, sh_ref, sc_ref, win_ref, cw_ref, cb_ref, wx_ref, bx_ref,
                wa_ref, ba_ref, ap_ref, o_ref, uext_ref, h_ref):
    ts, d = x_ref.shape

    @pl.when(pl.program_id(1) == 0)
    def _():
        uext_ref[0:SUBLANES, :] = jnp.zeros((SUBLANES, d), F32)
        h_ref[...] = jnp.zeros_like(h_ref)

    h = _norm_mod(x_ref[...], g_ref[...], sh_ref[0], sc_ref[0]).astype(BF16)
    yu = jnp.dot(h, win_ref[...], preferred_element_type=F32)
    y = yu[:, :d]
    u = yu[:, d:]
    uext_ref[SUBLANES:SUBLANES + ts, :] = u
    conv = cb_ref[...] + u * cw_ref[0:1, :]
    for tap in range(1, CONV_WIDTH):
        conv = conv + uext_ref[SUBLANES - tap:SUBLANES - tap + ts, :] * cw_ref[tap:tap + 1, :]
    uext_ref[0:SUBLANES, :] = u[ts - SUBLANES:ts, :]
    uc = conv
    ucb = uc.astype(BF16)
    nblk = wx_ref.shape[0]
    bw = d // nblk
    gx = jnp.concatenate([jnp.dot(ucb[:, n * bw:(n + 1) * bw], wx_ref[n], preferred_element_type=F32)
                          for n in range(nblk)], axis=1) + bx_ref[...]
    ga = jnp.concatenate([jnp.dot(ucb[:, n * bw:(n + 1) * bw], wa_ref[n], preferred_element_type=F32)
                          for n in range(nblk)], axis=1) + ba_ref[...]
    gate_x = jax.nn.sigmoid(gx)
    gate_a = jax.nn.sigmoid(ga)
    z = -ap_ref[...]
    softplus = jnp.maximum(z, 0.0) + jnp.log1p(jnp.exp(-jnp.abs(z)))
    log_a = (-LRU_C * gate_a) * softplus
    a = jnp.exp(log_a)
    b = jnp.sqrt(-jnp.tanh(log_a) * (1.0 + a * a)) * (gate_x * uc)
    ngrp = ts // SUBLANES
    a = a.reshape(ngrp, SUBLANES, d)
    b = b.reshape(ngrp, SUBLANES, d)
    sub = lax.broadcasted_iota(jnp.int32, (ngrp, SUBLANES, d), 1)
    step = 1
    while step < SUBLANES:
        keep = sub >= step
        a_prev = pltpu.roll(a, step, 1)
        b_prev = pltpu.roll(b, step, 1)
        b = jnp.where(keep, a * b_prev, 0.0) + b
        a = jnp.where(keep, a * a_prev, a)
        step *= 2
    carry = h_ref[...]
    groups = []
    for r in range(ngrp):
        hg = a[r] * carry + b[r]
        groups.append(hg)
        carry = hg[SUBLANES - 1:SUBLANES, :]
    hs = jnp.concatenate(groups, axis=0)
    h_ref[...] = carry
    gelu = 0.5 * y * (1.0 + jnp.tanh(math.sqrt(2.0 / math.pi) * (y + 0.044715 * (y * y * y))))
    o_ref[...] = (hs * gelu).astype(BF16)


def _lru(x, mod, layer, gain, w_in, conv_w, conv_b, w_x, b_x, w_a, b_a, a_param, *, seq, ts):
    t, d = x.shape
    bsz = t // seq
    ns = seq // ts
    vec = lambda v: v.reshape(1, d)
    return pl.pallas_call(
        _lru_kernel,
        out_shape=jax.ShapeDtypeStruct((t, d), BF16),
        grid=(bsz, ns),
        in_specs=[pl.BlockSpec((ts, d), lambda b, s: (b * ns + s, 0)),
                  pl.BlockSpec((1, d), lambda b, s: (0, 0)),
                  pl.BlockSpec((1, 1, d), lambda b, s: (layer * bsz + b, 0, 0)),
                  pl.BlockSpec((1, 1, d), lambda b, s: (layer * bsz + b, 0, 1)),
                  _resident(w_in.shape),
                  pl.BlockSpec((CONV_WIDTH, d), lambda b, s: (0, 0)),
                  pl.BlockSpec((1, d), lambda b, s: (0, 0)),
                  _resident(w_x.shape),
                  pl.BlockSpec((1, d), lambda b, s: (0, 0)),
                  _resident(w_a.shape),
                  pl.BlockSpec((1, d), lambda b, s: (0, 0)),
                  pl.BlockSpec((1, d), lambda b, s: (0, 0))],
        out_specs=pl.BlockSpec((ts, d), lambda b, s: (b * ns + s, 0)),
        scratch_shapes=[pltpu.VMEM((ts + SUBLANES, d), F32), pltpu.VMEM((1, d), F32)],
        compiler_params=_params("arbitrary", "arbitrary"),
        name="rglru_mixer",
    )(x, vec(gain), mod, mod, w_in, conv_w, vec(conv_b), w_x, vec(b_x), w_a, vec(b_a), vec(a_param))


def _post_mlp_kernel(x_ref, y_ref, wo_ref, g1_ref, g_ref, sh_ref, sc_ref, g2_ref, w1_ref, w2_ref,
                     o_ref, *, ff_chunk):
    x1 = x_ref[...] + g1_ref[0] * jnp.dot(y_ref[...], wo_ref[...], preferred_element_type=F32)
    h = _norm_mod(x1, g_ref[...], sh_ref[0], sc_ref[0]).astype(BF16)
    dff = w1_ref.shape[1]
    acc = None
    for c in range(dff // ff_chunk):
        a = jnp.dot(h, w1_ref[:, c * ff_chunk:(c + 1) * ff_chunk], preferred_element_type=F32)
        a = jnp.maximum(a, 0.0)
        a = (a * a).astype(BF16)
        part = jnp.dot(a, w2_ref[c * ff_chunk:(c + 1) * ff_chunk, :], preferred_element_type=F32)
        acc = part if acc is None else acc + part
    o_ref[...] = x1 + g2_ref[0] * acc


def _post_mlp(x, y, mod, layer, w_o, gain, w1, w2, *, seq, tm, ff_chunk):
    t, d = x.shape
    bsz = t // seq
    row = lambda i: layer * bsz + (i * tm) // seq
    part = lambda p: pl.BlockSpec((1, 1, d), lambda i: (row(i), 0, p))
    return pl.pallas_call(
        functools.partial(_post_mlp_kernel, ff_chunk=ff_chunk),
        out_shape=jax.ShapeDtypeStruct((t, d), F32),
        grid=(t // tm,),
        in_specs=[pl.BlockSpec((tm, d), lambda i: (i, 0)),
                  pl.BlockSpec((tm, d), lambda i: (i, 0)),
                  _resident(w_o.shape),
                  part(2),
                  pl.BlockSpec((1, d), lambda i: (0, 0)),
                  part(3), part(4), part(5),
                  _resident(w1.shape), _resident(w2.shape)],
        out_specs=pl.BlockSpec((tm, d), lambda i: (i, 0)),
        compiler_params=_params("arbitrary"),
        name="outproj_mlp",
    )(x, y, w_o, mod, gain.reshape(1, d), mod, mod, mod, w1, w2)


def kernel(x, c, rel_bias, w_ada, b_ada, norm_mix, norm_mlp, mlp_w1, mlp_w2, da_w_qkv, da_w_o, da_q_gain, da_k_gain, da_lam_q1, da_lam_k1, da_lam_q2, da_lam_k2, da_sub_gain, lru_w_in, lru_conv_w, lru_conv_b, lru_w_x, lru_b_x, lru_w_a, lru_b_a, lru_a_param, lru_w_out, dil_w_qkv, dil_w_o, dil_q_gain, dil_k_gain):
    bsz, seq, d = x.shape
    depth = w_ada.shape[0]
    t = bsz * seq
    nq = seq // TQ
    assert seq % TQ == 0 and d % LANES == 0

    mod = _ada(c, w_ada, b_ada).reshape(depth * bsz, 1, 6 * d)

    ncols = rel_bias.shape[1]
    rb_ext = jnp.concatenate(
        [rel_bias.T.astype(F32), jnp.full((ncols, 1), NEG_INF, F32),
         jnp.zeros((ncols, BUCKET_PAD - NUM_BUCKETS - 1), F32)], axis=1)
    kinds = [i % N_MIXERS for i in range(depth)]
    da_bias = _bias_tiles(rb_ext, nq, 1, None, seq) if 0 in kinds else None
    dil_bias = None
    if 2 in kinds:
        dil_bias = [_bias_tiles(rb_ext, 2 if seq // r >= 2 * TQ else 1, r, w // r, seq // r)
                    for (w, r) in DIL_PATTERNS]

    q_scale = HEAD_DIM ** -0.5 * LOG2E
    big_tile = min(BIG_TILE_ROWS, seq)
    xf = x.reshape(t, d)
    for i in range(depth):
        kind, slot = kinds[i], i // N_MIXERS
        if kind == 0:
            w = da_w_qkv[slot]
            wq = w[:, :d].T.astype(BF16)[None]
            wk = w[:, d:2 * d].astype(BF16)[None]
            wv = w[:, 2 * d:].T.astype(BF16)[None]
            qg = (da_q_gain[slot] * q_scale).reshape(HEAD_DIM, 1)
            kg = jnp.tile(da_k_gain[slot], 2).reshape(1, LANES)
            qt, k, vt = _qkv(xf, mod, i, norm_mix[i], wq, wk, wv, qg, kg, seq=seq, tm=big_tile,
                             dilations=(1,))
            lambda_init = 0.8 - 0.6 * math.exp(-0.3 * i)
            lamv = jnp.stack([da_lam_q1[slot], da_lam_k1[slot], da_lam_q2[slot], da_lam_k2[slot]]).astype(F32)
            y = _attention(qt, k, vt, [da_bias], _diff_plan(nq), seq=seq, diff=True, lambda_init=lambda_init,
                           lamv=lamv, subg=da_sub_gain[slot].reshape(LANES, 1))
            w_o = da_w_o[slot]
        elif kind == 1:
            y = _lru(xf, mod, i, norm_mix[i], lru_w_in[slot].astype(BF16), lru_conv_w[slot],
                     lru_conv_b[slot], lru_w_x[slot].astype(BF16), lru_b_x[slot],
                     lru_w_a[slot].astype(BF16), lru_b_a[slot], lru_a_param[slot], seq=seq, ts=big_tile)
            w_o = lru_w_out[slot]
        else:
            w = dil_w_qkv[slot].reshape(d, len(DIL_PATTERNS), 3, d)
            wq = jnp.transpose(w[:, :, 0, :], (1, 2, 0)).astype(BF16)
            wk = jnp.transpose(w[:, :, 1, :], (1, 0, 2)).astype(BF16)
            wv = jnp.transpose(w[:, :, 2, :], (1, 2, 0)).astype(BF16)
            qg = (dil_q_gain[slot] * q_scale).reshape(HEAD_DIM, 1)
            kg = jnp.tile(dil_k_gain[slot], 2).reshape(1, LANES)
            qt, k, vt = _qkv(xf, mod, i, norm_mix[i], wq, wk, wv, qg, kg, seq=seq, tm=TQ,
                             dilations=tuple(r for _, r in DIL_PATTERNS))
            y = _attention(qt, k, vt, dil_bias, _dilated_plan(seq, DIL_PATTERNS), seq=seq, diff=False)
            w_o = dil_w_o[slot]
        xf = _post_mlp(xf, y, mod, i, w_o.astype(BF16), norm_mlp[i], mlp_w1[i].astype(BF16),
                       mlp_w2[i].astype(BF16), seq=seq, tm=big_tile, ff_chunk=MLP_FF_CHUNK)
    return xf.reshape(bsz, seq, d)
```

```python
import functools
import math

import numpy as np
import jax
import jax.numpy as jnp
from jax import lax
from jax.experimental import pallas as pl
from jax.experimental.pallas import tpu as pltpu

F32 = jnp.float32
BF16 = jnp.bfloat16

RMS_EPS = 1e-6
NEG_INF = -1e30
NUM_BUCKETS = 32
REL_MAX_DISTANCE = 2048
HEAD_DIM = 64
N_MIXERS = 3
CONV_WIDTH = 4
LRU_C = 8.0
DIL_PATTERNS = ((128, 1), (512, 4), (2048, 16))
LOG2E = math.log2(math.e)
MASKED_BUCKET = NUM_BUCKETS
BUCKET_PAD = 64

LANES = 128
SUBLANES = 8
BF16_ROWS = 16
MXU_WIDTH = 256
VMEM_LIMIT_BYTES = 56 * 1024 * 1024

TQ = MXU_WIDTH
SCORE_LEAD = 5
ZERO_SHIFT = 20
BIG_TILE_ROWS = 4 * MXU_WIDTH
MLP_FF_CHUNK = 2 * MXU_WIDTH


def _params(*sem):
    return pltpu.CompilerParams(dimension_semantics=sem, vmem_limit_bytes=VMEM_LIMIT_BYTES)


def _resident(shape):
    nd = len(shape)
    return pl.BlockSpec(shape, lambda *_: (0,) * nd, pipeline_mode=pl.Buffered(1))


def _norm_mod(x, gain, shift, scale):
    ms = jnp.mean(x * x, axis=-1, keepdims=True)
    y = (x * lax.rsqrt(ms + RMS_EPS)) * gain
    return y * (1.0 + scale) + shift


def _ada_kernel(c_ref, w_ref, b_ref, o_ref):
    c = c_ref[...]
    a = (c * jax.nn.sigmoid(c)).astype(BF16)
    o_ref[0] = jnp.dot(a, w_ref[0].astype(BF16), preferred_element_type=F32) + b_ref[0]


def _ada(c, w_ada, b_ada):
    depth, d, n = w_ada.shape
    bsz = c.shape[0]
    tn = n // 4
    return pl.pallas_call(
        _ada_kernel,
        out_shape=jax.ShapeDtypeStruct((depth, bsz, n), F32),
        grid=(depth, n // tn),
        in_specs=[pl.BlockSpec((bsz, d), lambda l, j: (0, 0)),
                  pl.BlockSpec((1, d, tn), lambda l, j: (l, 0, j)),
                  pl.BlockSpec((1, 1, tn), lambda l, j: (l, 0, j))],
        out_specs=pl.BlockSpec((1, bsz, tn), lambda l, j: (l, 0, j)),
        compiler_params=_params("arbitrary", "arbitrary"),
        name="ada_mod",
    )(c, w_ada, b_ada.reshape(depth, 1, n))


def _t5_bucket_np(n):
    n = np.maximum(n, 0)
    max_exact = NUM_BUCKETS // 2
    nf = np.maximum(n, max_exact).astype(np.float32)
    large = max_exact + (np.log(nf / np.float32(max_exact)) / np.float32(math.log(REL_MAX_DISTANCE / max_exact))
                         * np.float32(NUM_BUCKETS - max_exact)).astype(np.int32)
    large = np.minimum(large, NUM_BUCKETS - 1)
    return np.where(n < max_exact, n, large)


def _bucket_tiles(nd, dilation, window, class_len):
    key = np.arange(TQ)[:, None]
    qry = np.arange(TQ)[None, :]
    tiles = []
    for delta in range(nd):
        steps = TQ * delta + qry - key
        valid = steps >= 0
        if window is not None:
            valid &= steps <= window
        if class_len < TQ:
            valid &= (key // class_len == qry // class_len) & (delta == 0)
        bucket = _t5_bucket_np(steps * dilation)
        tiles.append(np.where(valid, bucket, MASKED_BUCKET).reshape(1, TQ * TQ))
    return np.stack(tiles).astype(np.int32)


def _bias_kernel(bk_ref, rb_ref, o_ref):
    rb = rb_ref[...] * LOG2E
    hi = rb.astype(BF16)
    r1 = rb - hi.astype(F32)
    mid = r1.astype(BF16)
    low = (r1 - mid.astype(F32)).astype(BF16)
    n = o_ref.shape[2]
    chunk = 8192
    for c in range(n // chunk):
        bk = bk_ref[0, :, c * chunk:(c + 1) * chunk]
        onehot = lax.broadcasted_iota(jnp.int32, (BUCKET_PAD, chunk), 0) == bk
        oh = jnp.where(onehot, 1.0, 0.0).astype(BF16)
        o_ref[0, :, c * chunk:(c + 1) * chunk] = (
            jnp.dot(hi, oh, preferred_element_type=F32)
            + jnp.dot(mid, oh, preferred_element_type=F32)
            + jnp.dot(low, oh, preferred_element_type=F32))


def _bias_tiles(rb_ext, nd, dilation, window, class_len):
    buckets = jnp.asarray(_bucket_tiles(nd, dilation, window, class_len))
    ncols = rb_ext.shape[0]
    out = pl.pallas_call(
        _bias_kernel,
        out_shape=jax.ShapeDtypeStruct((nd, ncols, TQ * TQ), F32),
        grid=(nd,),
        in_specs=[pl.BlockSpec((1, 1, TQ * TQ), lambda i: (i, 0, 0)),
                  pl.BlockSpec((ncols, BUCKET_PAD), lambda i: (0, 0))],
        out_specs=pl.BlockSpec((1, ncols, TQ * TQ), lambda i: (i, 0, 0)),
        compiler_params=_params("arbitrary"),
        name="rel_bias_tiles",
    )(buckets, rb_ext)
    return out.reshape(nd, ncols, TQ, TQ)


def _class_major_rows(x_ref, u, tm, dilation):
    seq = x_ref.shape[0]
    class_len = seq // dilation
    if dilation == 1:
        return x_ref[pl.ds(pl.multiple_of(u * tm, tm), tm), :]
    if class_len >= tm:
        per_class = class_len // tm
        start = (u % per_class) * (tm * dilation) + u // per_class
        return x_ref[pl.ds(start, tm, stride=dilation), :]
    classes = tm // class_len
    return jnp.concatenate(
        [x_ref[pl.ds(u * classes + c, class_len, stride=dilation), :] for c in range(classes)], axis=0)


def _qkv_kernel(*refs, tm, dilations):
    nx = len(refs) - 11
    x_refs = refs[:nx]
    g_ref, sh_ref, sc_ref, wq_ref, wk_ref, wv_ref, qg_ref, kg_ref, qt_ref, k_ref, vt_ref = refs[nx:]
    groups = len(dilations)
    d = g_ref.shape[1]
    u = pl.program_id(0) % (x_refs[0].shape[0] // tm)
    nt = (((1,), (1,)), ((), ()))
    lo = lax.broadcasted_iota(jnp.int32, (tm, LANES), 1) < HEAD_DIM
    nsub = tm // TQ
    for g in range(groups):
        x = jnp.concatenate([_class_major_rows(xr, u, tm, dilations[g]) for xr in x_refs], axis=1)
        h = _norm_mod(x, g_ref[...], sh_ref[0], sc_ref[0]).astype(BF16)
        yk = jnp.dot(h, wk_ref[g], preferred_element_type=F32)
        for c in range(d // LANES):
            yc = yk[:, c * LANES:(c + 1) * LANES]
            y2 = yc * yc
            s_lo = jnp.sum(jnp.where(lo, y2, 0.0), axis=-1, keepdims=True)
            s_hi = jnp.sum(jnp.where(lo, 0.0, y2), axis=-1, keepdims=True)
            ms = jnp.where(lo, s_lo, s_hi) * (1.0 / HEAD_DIM)
            k_ref[g, :, c * LANES:(c + 1) * LANES] = (
                (yc * lax.rsqrt(ms + RMS_EPS)) * kg_ref[...]).astype(BF16)
        yq = lax.dot_general(wq_ref[g], h, nt, preferred_element_type=F32)
        y3 = yq.reshape(d // HEAD_DIM, HEAD_DIM, tm)
        ms = jnp.mean(y3 * y3, axis=1, keepdims=True)
        y3 = (y3 * lax.rsqrt(ms + RMS_EPS)) * qg_ref[...][None]
        yb = y3.reshape(d // LANES, LANES, tm).astype(BF16)
        for t in range(nsub):
            qt_ref[g, :, t] = yb[:, :, t * TQ:(t + 1) * TQ]
        yv = lax.dot_general(wv_ref[g], h, nt, preferred_element_type=F32)
        yvb = yv.reshape(d // LANES, LANES, tm).astype(BF16)
        for t in range(nsub):
            vt_ref[g, :, t] = yvb[:, :, t * TQ:(t + 1) * TQ]


def _qkv(x, mod, layer, gain, wq, wk, wv, qg, kg, *, seq, tm, dilations):
    t, d = x.shape
    groups = wq.shape[0]
    bsz = t // seq
    nblk = d // LANES
    row = lambda i: layer * bsz + (i * tm) // seq
    tshape = (groups, nblk, t // TQ, LANES, TQ)
    if max(dilations) > 1:
        x_specs = [pl.BlockSpec((seq, LANES), lambda i, j=j: ((i * tm) // seq, j)) for j in range(nblk)]
    else:
        x_specs = [pl.BlockSpec((tm, d), lambda i: (i, 0))]
    for r in dilations:
        assert (seq // r) % tm == 0 or (tm % (seq // r) == 0 and (seq // r) % SUBLANES == 0)
    return pl.pallas_call(
        functools.partial(_qkv_kernel, tm=tm, dilations=tuple(dilations)),
        out_shape=(jax.ShapeDtypeStruct(tshape, BF16),
                   jax.ShapeDtypeStruct((groups, t, d), BF16),
                   jax.ShapeDtypeStruct(tshape, BF16)),
        grid=(t // tm,),
        in_specs=x_specs + [
                  pl.BlockSpec((1, d), lambda i: (0, 0)),
                  pl.BlockSpec((1, 1, d), lambda i: (row(i), 0, 0)),
                  pl.BlockSpec((1, 1, d), lambda i: (row(i), 0, 1)),
                  _resident(wq.shape), _resident(wk.shape), _resident(wv.shape),
                  pl.BlockSpec((HEAD_DIM, 1), lambda i: (0, 0)),
                  pl.BlockSpec((1, LANES), lambda i: (0, 0))],
        out_specs=(pl.BlockSpec((groups, nblk, tm // TQ, LANES, TQ), lambda i: (0, 0, i, 0, 0)),
                   pl.BlockSpec((groups, tm, d), lambda i: (0, i, 0)),
                   pl.BlockSpec((groups, nblk, tm // TQ, LANES, TQ), lambda i: (0, 0, i, 0, 0))),
        compiler_params=_params("arbitrary"),
        name=f"qkv_proj_g{groups}",
    )(*([x] * len(x_specs)), gain.reshape(1, d), mod, mod, wq, wk, wv, qg, kg)


def _diff_plan(nq):
    return [(0, u, [(kj, u - kj) for kj in range(u + 1)], [(0, TQ, u * TQ, 1)]) for u in range(nq)]


def _dilated_plan(seq, patterns):
    plan = []
    for g, (window, r) in enumerate(patterns):
        class_len = seq // r
        assert window // r <= TQ
        for u in range(seq // TQ):
            if class_len >= TQ:
                per_class = class_len // TQ
                c, lt = u // per_class, u % per_class
                tiles = ([(u - 1, 1)] if lt > 0 else []) + [(u, 0)]
                stores = [(0, TQ, lt * TQ * r + c, r)]
            else:
                classes = TQ // class_len
                tiles = [(u, 0)]
                stores = [(h * class_len, class_len, u * classes + h, r) for h in range(classes)]
            plan.append((g, u, tiles, stores))
    return plan


def _attn_kernel(*refs, plan, nseg, diff, lambda_init):
    segs = [refs[4 * s:4 * s + 4] for s in range(nseg)]
    pos = 4 * nseg
    if diff:
        lamv_ref, subg_ref = refs[pos], refs[pos + 1]
        pos += 2
    o_ref, s_ref = refs[pos], refs[pos + 1]
    if not diff:
        og_ref, lse_ref = refs[pos + 2], refs[pos + 3]
    top = lax.broadcasted_iota(jnp.int32, (LANES, TQ), 0) < HEAD_DIM
    if diff:
        lv = lamv_ref[...]
        lam = (jnp.exp(jnp.sum(lv[0:1] * lv[1:2], axis=-1, keepdims=True))
               - jnp.exp(jnp.sum(lv[2:3] * lv[3:4], axis=-1, keepdims=True)) + lambda_init)

    ones_rows = jnp.ones((BF16_ROWS, TQ), BF16)
    dyn0 = lax.shift_right_logical(pl.program_id(1), ZERO_SHIFT)
    state = {}
    slot = {}

    def tiles_of(i):
        return plan[i][2]

    def rows(ref, g, start, n, stride):
        return ref.at[g, pl.ds(start, n, stride=stride) if stride > 1 else pl.ds(start, n), :]

    def scores(i, n):
        st_i = state.setdefault(i, {"m": [None, None], "mh": ([], []), "acc": [None, None], "q": None})
        g, u, tiles, _ = plan[i]
        kj, bidx = tiles[n]
        qt_ref, k_ref, _, b_ref = segs[g]
        if st_i["q"] is None:
            qt = qt_ref[0, 0, u]
            zero = jnp.zeros_like(qt)
            st_i["q"] = (jnp.where(top, qt, zero), jnp.where(top, zero, qt))
        kk = k_ref[0, kj * TQ:(kj + 1) * TQ, :]
        for st in range(2):
            sc = jnp.dot(kk, st_i["q"][st], preferred_element_type=F32) + b_ref[bidx, st]
            s_ref[slot[(i, n)] + dyn0, st] = sc
            cm = jnp.max(sc, axis=0, keepdims=True)
            st_i["m"][st] = cm if st_i["m"][st] is None else jnp.maximum(st_i["m"][st], cm)
            st_i["mh"][st].append(st_i["m"][st])

    def weigh(i, n):
        st_i = state[i]
        g, _, tiles, _ = plan[i]
        vt1 = jnp.concatenate([segs[g][2][0, 0, tiles[n][0]], ones_rows], axis=0)
        for st in range(2):
            mh = st_i["mh"][st]
            p = jnp.exp2(s_ref[slot[(i, n)] + dyn0, st] - mh[n])
            pv = jnp.dot(vt1, p.astype(BF16), preferred_element_type=F32)
            st_i["acc"][st] = pv if n == 0 else jnp.exp2(mh[n - 1] - mh[n]) * st_i["acc"][st] + pv

    def finish(i):
        g, _, _, stores = plan[i]
        st_i = state.pop(i)
        acc, m = st_i["acc"], st_i["m"]
        la, lb = acc[0][LANES:LANES + 1], acc[1][LANES:LANES + 1]
        oa = acc[0][:LANES] * (1.0 / la)
        ob = acc[1][:LANES] * (1.0 / lb)
        if diff:
            ot = oa - lam * ob
            ms = jnp.mean(ot * ot, axis=0, keepdims=True)
            ot = ((ot * lax.rsqrt(ms + RMS_EPS)) * subg_ref[...]) * (1.0 - lambda_init)
            for row0, n, dst, stride in stores:
                assert stride == 1
                o_ref[dst:dst + n, :] = ot.T[row0:row0 + n].astype(BF16)
        else:
            o_rows = jnp.where(top, oa, ob).T
            lse_rows = jnp.where(top, m[0] + jnp.log2(la), m[1] + jnp.log2(lb)).T
            for row0, n, dst, stride in stores:
                rows(og_ref, g, dst, n, stride)[...] = o_rows[row0:row0 + n]
                rows(lse_ref, g, dst, n, stride)[...] = lse_rows[row0:row0 + n]

    def chunks_of(i):
        return {(dst + t * stride) // TQ for _, n, dst, stride in plan[i][3] for t in range(n)}

    nchunk = o_ref.shape[0] // TQ
    if diff:
        by_size = sorted(range(len(plan)), key=lambda i: len(tiles_of(i)))
        order = by_size[0::2] + by_size[1::2][::-1]
    else:
        order = []
        for c in range(nchunk):
            order += [i for i in range(len(plan)) if c in chunks_of(i) and i not in order]
    needs = {c: {i for i in range(len(plan)) if c in chunks_of(i)} for c in range(nchunk)}
    finished, merged = set(), set()

    def merge_ready():
        for c in range(nchunk):
            if c in merged or not needs[c] <= finished:
                continue
            merged.add(c)
            r0 = c * TQ
            lses = [lse_ref[g, r0:r0 + TQ, :] for g in range(nseg)]
            top_lse = functools.reduce(jnp.maximum, lses)
            ws = [jnp.exp2(l - top_lse) for l in lses]
            num = sum(w * og_ref[g, r0:r0 + TQ, :] for g, w in enumerate(ws))
            o_ref[r0:r0 + TQ, :] = (num * (1.0 / sum(ws))).astype(BF16)

    flat = [(i, n) for i in order for n in range(len(tiles_of(i)))]
    assert s_ref.shape[0] > SCORE_LEAD
    for t, (i, n) in enumerate(flat):
        slot[(i, n)] = t % s_ref.shape[0]
    issued = 0
    for t, (i, n) in enumerate(flat):
        while issued <= min(t + SCORE_LEAD, len(flat) - 1):
            scores(*flat[issued])
            issued += 1
        weigh(i, n)
        if n == len(tiles_of(i)) - 1:
            finish(i)
            finished.add(i)
            if not diff:
                merge_ready()
    assert diff or len(merged) == nchunk


def _attention(qt, k, vt, biases, plan, *, seq, diff, lambda_init=0.0, lamv=None, subg=None):
    groups, nblk, ntile, _, _ = qt.shape
    t, d = k.shape[1], k.shape[2]
    bsz = t // seq
    nq = seq // TQ
    nds = tuple(b.shape[0] for b in biases)
    assert bsz < 2 ** ZERO_SHIFT
    scratch = [pltpu.VMEM((SCORE_LEAD + 2, 2, TQ, TQ), F32)]
    if not diff:
        scratch += [pltpu.VMEM((groups, seq, LANES), F32), pltpu.VMEM((groups, seq, LANES), F32)]
    in_specs, args = [], []
    for g in range(groups):
        in_specs += [
            pl.BlockSpec((1, 1, nq, LANES, TQ), lambda j, b, g=g: (g, j, b, 0, 0)),
            pl.BlockSpec((1, seq, LANES), lambda j, b, g=g: (g, b, j)),
            pl.BlockSpec((1, 1, nq, LANES, TQ), lambda j, b, g=g: (g, j, b, 0, 0)),
            pl.BlockSpec((nds[g], 2, TQ, TQ), lambda j, b: (0, j, 0, 0)),
        ]
        args += [qt, k, vt, biases[g]]
    if diff:
        in_specs += [pl.BlockSpec(lamv.shape, lambda j, b: (0, 0)),
                     pl.BlockSpec((LANES, 1), lambda j, b: (0, 0))]
        args += [lamv, subg]
    return pl.pallas_call(
        functools.partial(_attn_kernel, plan=plan, nseg=groups, diff=diff, lambda_init=lambda_init),
        out_shape=jax.ShapeDtypeStruct((t, d), BF16),
        grid=(nblk, bsz),
        in_specs=in_specs,
        out_specs=pl.BlockSpec((seq, LANES), lambda j, b: (b, j)),
        scratch_shapes=scratch,
        compiler_params=_params("arbitrary", "arbitrary"),
        name="diff_attention" if diff else "dilated_attention",
    )(*args)


def _lru_kernel(x_ref, g_ref, sh_ref, sc_ref, win_ref, cw_ref, cb_ref, wx_ref, bx_ref,
                wa_ref, ba_ref, ap_ref, o_ref, uext_ref, h_ref):
    ts, d = x_ref.shape

    @pl.when(pl.program_id(1) == 0)
    def _():
        uext_ref[0:SUBLANES, :] = jnp.zeros((SUBLANES, d), F32)
        h_ref[...] = jnp.zeros_like(h_ref)

    h = _norm_mod(x_ref[...], g_ref[...], sh_ref[0], sc_ref[0]).astype(BF16)
    yu = jnp.dot(h, win_ref[...], preferred_element_type=F32)
    y = yu[:, :d]
    u = yu[:, d:]
    uext_ref[SUBLANES:SUBLANES + ts, :] = u
    conv = cb_ref[...] + u * cw_ref[0:1, :]
    for tap in range(1, CONV_WIDTH):
        conv = conv + uext_ref[SUBLANES - tap:SUBLANES - tap + ts, :] * cw_ref[tap:tap + 1, :]
    uext_ref[0:SUBLANES, :] = u[ts - SUBLANES:ts, :]
    uc = conv
    ucb = uc.astype(BF16)
    nblk = wx_ref.shape[0]
    bw = d // nblk
    gx = jnp.concatenate([jnp.dot(ucb[:, n * bw:(n + 1) * bw], wx_ref[n], preferred_element_type=F32)
                          for n in range(nblk)], axis=1) + bx_ref[...]
    ga = jnp.concatenate([jnp.dot(ucb[:, n * bw:(n + 1) * bw], wa_ref[n], preferred_element_type=F32)
                          for n in range(nblk)], axis=1) + ba_ref[...]
    gate_x = 1.0 / (1.0 + jnp.exp2(gx * -LOG2E))
    gate_a = 1.0 / (1.0 + jnp.exp2(ga * -LOG2E))
    z = -ap_ref[...]
    softplus = jnp.maximum(z, 0.0) + jnp.log1p(jnp.exp(-jnp.abs(z)))
    rate = -LRU_C * softplus
    log_a = gate_a * rate
    a = jnp.exp2(gate_a * (rate * LOG2E))
    t = -jnp.tanh(log_a) * (1.0 + a * a)
    b = jnp.where(t > 0.0, t * lax.rsqrt(t), 0.0) * (gate_x * uc)
    ngrp = ts // SUBLANES
    a = a.reshape(ngrp, SUBLANES, d)
    b = b.reshape(ngrp, SUBLANES, d)
    sub = lax.broadcasted_iota(jnp.int32, (ngrp, SUBLANES, d), 1)
    step = 1
    while step < SUBLANES:
        keep = sub >= step
        a_prev = pltpu.roll(a, step, 1)
        b_prev = pltpu.roll(b, step, 1)
        b = jnp.where(keep, a * b_prev, 0.0) + b
        a = jnp.where(keep, a * a_prev, a)
        step *= 2
    carry = h_ref[...]
    groups = []
    for r in range(ngrp):
        hg = a[r] * carry + b[r]
        groups.append(hg)
        carry = hg[SUBLANES - 1:SUBLANES, :]
    hs = jnp.concatenate(groups, axis=0)
    h_ref[...] = carry
    gelu = 0.5 * y * (1.0 + jnp.tanh(math.sqrt(2.0 / math.pi) * (y + 0.044715 * (y * y * y))))
    o_ref[...] = (hs * gelu).astype(BF16)


def _lru(x, mod, layer, gain, w_in, conv_w, conv_b, w_x, b_x, w_a, b_a, a_param, *, seq, ts):
    t, d = x.shape
    bsz = t // seq
    ns = seq // ts
    vec = lambda v: v.reshape(1, d)
    return pl.pallas_call(
        _lru_kernel,
        out_shape=jax.ShapeDtypeStruct((t, d), BF16),
        grid=(bsz, ns),
        in_specs=[pl.BlockSpec((ts, d), lambda b, s: (b * ns + s, 0)),
                  pl.BlockSpec((1, d), lambda b, s: (0, 0)),
                  pl.BlockSpec((1, 1, d), lambda b, s: (layer * bsz + b, 0, 0)),
                  pl.BlockSpec((1, 1, d), lambda b, s: (layer * bsz + b, 0, 1)),
                  _resident(w_in.shape),
                  pl.BlockSpec((CONV_WIDTH, d), lambda b, s: (0, 0)),
                  pl.BlockSpec((1, d), lambda b, s: (0, 0)),
                  _resident(w_x.shape),
                  pl.BlockSpec((1, d), lambda b, s: (0, 0)),
                  _resident(w_a.shape),
                  pl.BlockSpec((1, d), lambda b, s: (0, 0)),
                  pl.BlockSpec((1, d), lambda b, s: (0, 0))],
        out_specs=pl.BlockSpec((ts, d), lambda b, s: (b * ns + s, 0)),
        scratch_shapes=[pltpu.VMEM((ts + SUBLANES, d), F32), pltpu.VMEM((1, d), F32)],
        compiler_params=_params("arbitrary", "arbitrary"),
        name="rglru_mixer",
    )(x, vec(gain), mod, mod, w_in, conv_w, vec(conv_b), w_x, vec(b_x), w_a, vec(b_a), vec(a_param))


def _post_mlp_kernel(x_ref, y_ref, wo_ref, g1_ref, g_ref, sh_ref, sc_ref, g2_ref, w1_ref, w2_ref,
                     o_ref, *, ff_chunk):
    x1 = x_ref[...] + g1_ref[0] * jnp.dot(y_ref[...], wo_ref[...], preferred_element_type=F32)
    h = _norm_mod(x1, g_ref[...], sh_ref[0], sc_ref[0]).astype(BF16)
    dff = w1_ref.shape[1]
    acc = None
    for c in range(dff // ff_chunk):
        a = jnp.dot(h, w1_ref[:, c * ff_chunk:(c + 1) * ff_chunk], preferred_element_type=F32)
        a = jnp.maximum(a, 0.0)
        a = (a * a).astype(BF16)
        part = jnp.dot(a, w2_ref[c * ff_chunk:(c + 1) * ff_chunk, :], preferred_element_type=F32)
        acc = part if acc is None else acc + part
    o_ref[...] = x1 + g2_ref[0] * acc


def _post_mlp(x, y, mod, layer, w_o, gain, w1, w2, *, seq, tm, ff_chunk):
    t, d = x.shape
    bsz = t // seq
    row = lambda i: layer * bsz + (i * tm) // seq
    part = lambda p: pl.BlockSpec((1, 1, d), lambda i: (row(i), 0, p))
    return pl.pallas_call(
        functools.partial(_post_mlp_kernel, ff_chunk=ff_chunk),
        out_shape=jax.ShapeDtypeStruct((t, d), F32),
        grid=(t // tm,),
        in_specs=[pl.BlockSpec((tm, d), lambda i: (i, 0)),
                  pl.BlockSpec((tm, d), lambda i: (i, 0)),
                  _resident(w_o.shape),
                  part(2),
                  pl.BlockSpec((1, d), lambda i: (0, 0)),
                  part(3), part(4), part(5),
                  _resident(w1.shape), _resident(w2.shape)],
        out_specs=pl.BlockSpec((tm, d), lambda i: (i, 0)),
        compiler_params=_params("arbitrary"),
        name="outproj_mlp",
    )(x, y, w_o, mod, gain.reshape(1, d), mod, mod, mod, w1, w2)


def kernel(x, c, rel_bias, w_ada, b_ada, norm_mix, norm_mlp, mlp_w1, mlp_w2, da_w_qkv, da_w_o, da_q_gain, da_k_gain, da_lam_q1, da_lam_k1, da_lam_q2, da_lam_k2, da_sub_gain, lru_w_in, lru_conv_w, lru_conv_b, lru_w_x, lru_b_x, lru_w_a, lru_b_a, lru_a_param, lru_w_out, dil_w_qkv, dil_w_o, dil_q_gain, dil_k_gain):
    bsz, seq, d = x.shape
    depth = w_ada.shape[0]
    t = bsz * seq
    nq = seq // TQ
    assert seq % TQ == 0 and d % LANES == 0

    mod = _ada(c, w_ada, b_ada).reshape(depth * bsz, 1, 6 * d)

    ncols = rel_bias.shape[1]
    rb_ext = jnp.concatenate(
        [rel_bias.T.astype(F32), jnp.full((ncols, 1), NEG_INF, F32),
         jnp.zeros((ncols, BUCKET_PAD - NUM_BUCKETS - 1), F32)], axis=1)
    kinds = [i % N_MIXERS for i in range(depth)]
    da_bias = _bias_tiles(rb_ext, nq, 1, None, seq) if 0 in kinds else None
    dil_bias = None
    if 2 in kinds:
        dil_bias = [_bias_tiles(rb_ext, 2 if seq // r >= 2 * TQ else 1, r, w // r, seq // r)
                    for (w, r) in DIL_PATTERNS]

    q_scale = HEAD_DIM ** -0.5 * LOG2E
    big_tile = min(BIG_TILE_ROWS, seq)
    xf = x.reshape(t, d)
    for i in range(depth):
        kind, slot = kinds[i], i // N_MIXERS
        if kind == 0:
            w = da_w_qkv[slot]
            wq = w[:, :d].T.astype(BF16)[None]
            wk = w[:, d:2 * d].astype(BF16)[None]
            wv = w[:, 2 * d:].T.astype(BF16)[None]
            qg = (da_q_gain[slot] * q_scale).reshape(HEAD_DIM, 1)
            kg = jnp.tile(da_k_gain[slot], 2).reshape(1, LANES)
            qt, k, vt = _qkv(xf, mod, i, norm_mix[i], wq, wk, wv, qg, kg, seq=seq, tm=big_tile,
                             dilations=(1,))
            lambda_init = 0.8 - 0.6 * math.exp(-0.3 * i)
            lamv = jnp.stack([da_lam_q1[slot], da_lam_k1[slot], da_lam_q2[slot], da_lam_k2[slot]]).astype(F32)
            y = _attention(qt, k, vt, [da_bias], _diff_plan(nq), seq=seq, diff=True, lambda_init=lambda_init,
                           lamv=lamv, subg=da_sub_gain[slot].reshape(LANES, 1))
            w_o = da_w_o[slot]
        elif kind == 1:
            y = _lru(xf, mod, i, norm_mix[i], lru_w_in[slot].astype(BF16), lru_conv_w[slot],
                     lru_conv_b[slot], lru_w_x[slot].astype(BF16), lru_b_x[slot],
                     lru_w_a[slot].astype(BF16), lru_b_a[slot], lru_a_param[slot], seq=seq, ts=big_tile)
            w_o = lru_w_out[slot]
        else:
            w = dil_w_qkv[slot].reshape(d, len(DIL_PATTERNS), 3, d)
            wq = jnp.transpose(w[:, :, 0, :], (1, 2, 0)).astype(BF16)
            wk = jnp.transpose(w[:, :, 1, :], (1, 0, 2)).astype(BF16)
            wv = jnp.transpose(w[:, :, 2, :], (1, 2, 0)).astype(BF16)
            qg = (dil_q_gain[slot] * q_scale).reshape(HEAD_DIM, 1)
            kg = jnp.tile(dil_k_gain[slot], 2).reshape(1, LANES)
            qt, k, vt = _qkv(xf, mod, i, norm_mix[i], wq, wk, wv, qg, kg, seq=seq, tm=TQ,
                             dilations=tuple(r for _, r in DIL_PATTERNS))
            y = _attention(qt, k, vt, dil_bias, _dilated_plan(seq, DIL_PATTERNS), seq=seq, diff=False)
            w_o = dil_w_o[slot]
        xf = _post_mlp(xf, y, mod, i, w_o.astype(BF16), norm_mlp[i], mlp_w1[i].astype(BF16),
                       mlp_w2[i].astype(BF16), seq=seq, tm=big_tile, ff_chunk=MLP_FF_CHUNK)
    return xf.reshape(bsz, seq, d)
```

```python
import functools
import math

import numpy as np
import jax
import jax.numpy as jnp
from jax import lax
from jax.experimental import pallas as pl
from jax.experimental.pallas import tpu as pltpu

F32 = jnp.float32
BF16 = jnp.bfloat16

RMS_EPS = 1e-6
NEG_INF = -1e30
NUM_BUCKETS = 32
REL_MAX_DISTANCE = 2048
HEAD_DIM = 64
N_MIXERS = 3
CONV_WIDTH = 4
LRU_C = 8.0
DIL_PATTERNS = ((128, 1), (512, 4), (2048, 16))
LOG2E = math.log2(math.e)
MASKED_BUCKET = NUM_BUCKETS
BUCKET_PAD = 64

LANES = 128
SUBLANES = 8
BF16_ROWS = 16
MXU_WIDTH = 256
VMEM_LIMIT_BYTES = 56 * 1024 * 1024

TQ = MXU_WIDTH
SCORE_LEAD = 5
ZERO_SHIFT = 20
BIG_TILE_ROWS = 4 * MXU_WIDTH
MLP_FF_CHUNK = 2 * MXU_WIDTH


def _params(*sem):
    return pltpu.CompilerParams(dimension_semantics=sem, vmem_limit_bytes=VMEM_LIMIT_BYTES)


def _resident(shape):
    nd = len(shape)
    return pl.BlockSpec(shape, lambda *_: (0,) * nd, pipeline_mode=pl.Buffered(1))


def _norm_mod(x, gain, shift, scale):
    ms = jnp.mean(x * x, axis=-1, keepdims=True)
    y = (x * lax.rsqrt(ms + RMS_EPS)) * gain
    return y * (1.0 + scale) + shift


def _ada_kernel(c_ref, w_ref, b_ref, o_ref):
    c = c_ref[...]
    a = (c * jax.nn.sigmoid(c)).astype(BF16)
    o_ref[0] = jnp.dot(a, w_ref[0].astype(BF16), preferred_element_type=F32) + b_ref[0]


def _ada(c, w_ada, b_ada):
    depth, d, n = w_ada.shape
    bsz = c.shape[0]
    tn = n // 4
    return pl.pallas_call(
        _ada_kernel,
        out_shape=jax.ShapeDtypeStruct((depth, bsz, n), F32),
        grid=(depth, n // tn),
        in_specs=[pl.BlockSpec((bsz, d), lambda l, j: (0, 0)),
                  pl.BlockSpec((1, d, tn), lambda l, j: (l, 0, j)),
                  pl.BlockSpec((1, 1, tn), lambda l, j: (l, 0, j))],
        out_specs=pl.BlockSpec((1, bsz, tn), lambda l, j: (l, 0, j)),
        compiler_params=_params("arbitrary", "arbitrary"),
        name="ada_mod",
    )(c, w_ada, b_ada.reshape(depth, 1, n))


def _t5_bucket_np(n):
    n = np.maximum(n, 0)
    max_exact = NUM_BUCKETS // 2
    nf = np.maximum(n, max_exact).astype(np.float32)
    large = max_exact + (np.log(nf / np.float32(max_exact)) / np.float32(math.log(REL_MAX_DISTANCE / max_exact))
                         * np.float32(NUM_BUCKETS - max_exact)).astype(np.int32)
    large = np.minimum(large, NUM_BUCKETS - 1)
    return np.where(n < max_exact, n, large)


def _bucket_tiles(nd, dilation, window, class_len):
    key = np.arange(TQ)[:, None]
    qry = np.arange(TQ)[None, :]
    tiles = []
    for delta in range(nd):
        steps = TQ * delta + qry - key
        valid = steps >= 0
        if window is not None:
            valid &= steps <= window
        if class_len < TQ:
            valid &= (key // class_len == qry // class_len) & (delta == 0)
        bucket = _t5_bucket_np(steps * dilation)
        tiles.append(np.where(valid, bucket, MASKED_BUCKET).reshape(1, TQ * TQ))
    return np.stack(tiles).astype(np.int32)


def _bias_kernel(bk_ref, rb_ref, o_ref):
    rb = rb_ref[...] * LOG2E
    hi = rb.astype(BF16)
    r1 = rb - hi.astype(F32)
    mid = r1.astype(BF16)
    low = (r1 - mid.astype(F32)).astype(BF16)
    n = o_ref.shape[2]
    chunk = 8192
    for c in range(n // chunk):
        bk = bk_ref[0, :, c * chunk:(c + 1) * chunk]
        onehot = lax.broadcasted_iota(jnp.int32, (BUCKET_PAD, chunk), 0) == bk
        oh = jnp.where(onehot, 1.0, 0.0).astype(BF16)
        o_ref[0, :, c * chunk:(c + 1) * chunk] = (
            jnp.dot(hi, oh, preferred_element_type=F32)
            + jnp.dot(mid, oh, preferred_element_type=F32)
            + jnp.dot(low, oh, preferred_element_type=F32))


def _bias_tiles(rb_ext, nd, dilation, window, class_len):
    buckets = jnp.asarray(_bucket_tiles(nd, dilation, window, class_len))
    ncols = rb_ext.shape[0]
    out = pl.pallas_call(
        _bias_kernel,
        out_shape=jax.ShapeDtypeStruct((nd, ncols, TQ * TQ), F32),
        grid=(nd,),
        in_specs=[pl.BlockSpec((1, 1, TQ * TQ), lambda i: (i, 0, 0)),
                  pl.BlockSpec((ncols, BUCKET_PAD), lambda i: (0, 0))],
        out_specs=pl.BlockSpec((1, ncols, TQ * TQ), lambda i: (i, 0, 0)),
        compiler_params=_params("arbitrary"),
        name="rel_bias_tiles",
    )(buckets, rb_ext)
    return out.reshape(nd, ncols, TQ, TQ)


def _class_major_rows(x_ref, u, tm, dilation):
    seq = x_ref.shape[0]
    class_len = seq // dilation
    if dilation == 1:
        return x_ref[pl.ds(pl.multiple_of(u * tm, tm), tm), :]
    if class_len >= tm:
        per_class = class_len // tm
        start = (u % per_class) * (tm * dilation) + u // per_class
        return x_ref[pl.ds(start, tm, stride=dilation), :]
    classes = tm // class_len
    return jnp.concatenate(
        [x_ref[pl.ds(u * classes + c, class_len, stride=dilation), :] for c in range(classes)], axis=0)


def _qkv_kernel(*refs, tm, dilations):
    nx = len(refs) - 11
    x_refs = refs[:nx]
    g_ref, sh_ref, sc_ref, wq_ref, wk_ref, wv_ref, qg_ref, kg_ref, qt_ref, k_ref, vt_ref = refs[nx:]
    groups = len(dilations)
    d = g_ref.shape[1]
    u = pl.program_id(0) % (x_refs[0].shape[0] // tm)
    nt = (((0 if groups == 1 else 1,), (1,)), ((), ()))
    lo = lax.broadcasted_iota(jnp.int32, (tm, LANES), 1) < HEAD_DIM
    nsub = tm // TQ
    for g in range(groups):
        x = jnp.concatenate([_class_major_rows(xr, u, tm, dilations[g]) for xr in x_refs], axis=1)
        h = _norm_mod(x, g_ref[...], sh_ref[0], sc_ref[0]).astype(BF16)
        yk = jnp.dot(h, wk_ref[g], preferred_element_type=F32)
        for c in range(d // LANES):
            yc = yk[:, c * LANES:(c + 1) * LANES]
            y2 = yc * yc
            s_lo = jnp.sum(jnp.where(lo, y2, 0.0), axis=-1, keepdims=True)
            s_hi = jnp.sum(jnp.where(lo, 0.0, y2), axis=-1, keepdims=True)
            ms = jnp.where(lo, s_lo, s_hi) * (1.0 / HEAD_DIM)
            k_ref[g, :, c * LANES:(c + 1) * LANES] = (
                (yc * lax.rsqrt(ms + RMS_EPS)) * kg_ref[...]).astype(BF16)
        yq = lax.dot_general(wq_ref[g], h, nt, preferred_element_type=F32)
        y3 = yq.reshape(d // HEAD_DIM, HEAD_DIM, tm)
        ms = jnp.mean(y3 * y3, axis=1, keepdims=True)
        y3 = (y3 * lax.rsqrt(ms + RMS_EPS)) * qg_ref[...][None]
        yb = y3.reshape(d // LANES, LANES, tm).astype(BF16)
        for t in range(nsub):
            qt_ref[g, :, t] = yb[:, :, t * TQ:(t + 1) * TQ]
        yv = lax.dot_general(wv_ref[g], h, nt, preferred_element_type=F32)
        yvb = yv.reshape(d // LANES, LANES, tm).astype(BF16)
        for t in range(nsub):
            vt_ref[g, :, t] = yvb[:, :, t * TQ:(t + 1) * TQ]


def _qkv(x, mod, layer, gain, wq, wk, wv, qg, kg, *, seq, tm, dilations):
    t, d = x.shape
    groups = wq.shape[0]
    bsz = t // seq
    nblk = d // LANES
    row = lambda i: layer * bsz + (i * tm) // seq
    tshape = (groups, nblk, t // TQ, LANES, TQ)
    if max(dilations) > 1:
        x_specs = [pl.BlockSpec((seq, LANES), lambda i, j=j: ((i * tm) // seq, j)) for j in range(nblk)]
    else:
        x_specs = [pl.BlockSpec((tm, d), lambda i: (i, 0))]
    for r in dilations:
        assert (seq // r) % tm == 0 or (tm % (seq // r) == 0 and (seq // r) % SUBLANES == 0)
    return pl.pallas_call(
        functools.partial(_qkv_kernel, tm=tm, dilations=tuple(dilations)),
        out_shape=(jax.ShapeDtypeStruct(tshape, BF16),
                   jax.ShapeDtypeStruct((groups, t, d), BF16),
                   jax.ShapeDtypeStruct(tshape, BF16)),
        grid=(t // tm,),
        in_specs=x_specs + [
                  pl.BlockSpec((1, d), lambda i: (0, 0)),
                  pl.BlockSpec((1, 1, d), lambda i: (row(i), 0, 0)),
                  pl.BlockSpec((1, 1, d), lambda i: (row(i), 0, 1)),
                  _resident(wq.shape), _resident(wk.shape), _resident(wv.shape),
                  pl.BlockSpec((HEAD_DIM, 1), lambda i: (0, 0)),
                  pl.BlockSpec((1, LANES), lambda i: (0, 0))],
        out_specs=(pl.BlockSpec((groups, nblk, tm // TQ, LANES, TQ), lambda i: (0, 0, i, 0, 0)),
                   pl.BlockSpec((groups, tm, d), lambda i: (0, i, 0)),
                   pl.BlockSpec((groups, nblk, tm // TQ, LANES, TQ), lambda i: (0, 0, i, 0, 0))),
        compiler_params=_params("arbitrary"),
        name=f"qkv_proj_g{groups}",
    )(*([x] * len(x_specs)), gain.reshape(1, d), mod, mod, wq, wk, wv, qg, kg)


def _diff_plan(nq):
    return [(0, u, [(kj, u - kj) for kj in range(u + 1)], [(0, TQ, u * TQ, 1)]) for u in range(nq)]


def _dilated_plan(seq, patterns):
    plan = []
    for g, (window, r) in enumerate(patterns):
        class_len = seq // r
        assert window // r <= TQ
        for u in range(seq // TQ):
            if class_len >= TQ:
                per_class = class_len // TQ
                c, lt = u // per_class, u % per_class
                tiles = ([(u - 1, 1)] if lt > 0 else []) + [(u, 0)]
                stores = [(0, TQ, lt * TQ * r + c, r)]
            else:
                classes = TQ // class_len
                tiles = [(u, 0)]
                stores = [(h * class_len, class_len, u * classes + h, r) for h in range(classes)]
            plan.append((g, u, tiles, stores))
    return plan


def _attn_kernel(*refs, plan, nseg, diff, lambda_init):
    segs = [refs[4 * s:4 * s + 4] for s in range(nseg)]
    pos = 4 * nseg
    if diff:
        lamv_ref, subg_ref = refs[pos], refs[pos + 1]
        pos += 2
    o_ref, s_ref = refs[pos], refs[pos + 1]
    if not diff:
        og_ref, lse_ref = refs[pos + 2], refs[pos + 3]
    top = lax.broadcasted_iota(jnp.int32, (LANES, TQ), 0) < HEAD_DIM
    if diff:
        lv = lamv_ref[...]
        lam = (jnp.exp(jnp.sum(lv[0:1] * lv[1:2], axis=-1, keepdims=True))
               - jnp.exp(jnp.sum(lv[2:3] * lv[3:4], axis=-1, keepdims=True)) + lambda_init)

    ones_rows = jnp.ones((BF16_ROWS, TQ), BF16)
    dyn0 = lax.shift_right_logical(pl.program_id(1), ZERO_SHIFT)
    state = {}
    slot = {}

    def tiles_of(i):
        return plan[i][2]

    def rows(ref, g, start, n, stride):
        return ref.at[g, pl.ds(start, n, stride=stride) if stride > 1 else pl.ds(start, n), :]

    def scores(i, n):
        st_i = state.setdefault(i, {"m": [None, None], "mh": ([], []), "acc": [None, None], "q": None})
        g, u, tiles, _ = plan[i]
        kj, bidx = tiles[n]
        qt_ref, k_ref, _, b_ref = segs[g]
        if st_i["q"] is None:
            qt = qt_ref[0, 0, u]
            zero = jnp.zeros_like(qt)
            st_i["q"] = (jnp.where(top, qt, zero), jnp.where(top, zero, qt))
        kk = k_ref[0, kj * TQ:(kj + 1) * TQ, :]
        for st in range(2):
            sc = jnp.dot(kk, st_i["q"][st], preferred_element_type=F32) + b_ref[bidx, st]
            s_ref[slot[(i, n)] + dyn0, st] = sc
            cm = jnp.max(sc, axis=0, keepdims=True)
            st_i["m"][st] = cm if st_i["m"][st] is None else jnp.maximum(st_i["m"][st], cm)
            st_i["mh"][st].append(st_i["m"][st])

    def weigh(i, n):
        st_i = state[i]
        g, _, tiles, _ = plan[i]
        vt1 = jnp.concatenate([segs[g][2][0, 0, tiles[n][0]], ones_rows], axis=0)
        for st in range(2):
            mh = st_i["mh"][st]
            p = jnp.exp2(s_ref[slot[(i, n)] + dyn0, st] - mh[n])
            pv = jnp.dot(vt1, p.astype(BF16), preferred_element_type=F32)
            st_i["acc"][st] = pv if n == 0 else jnp.exp2(mh[n - 1] - mh[n]) * st_i["acc"][st] + pv

    def finish(i):
        g, _, _, stores = plan[i]
        st_i = state.pop(i)
        acc, m = st_i["acc"], st_i["m"]
        la, lb = acc[0][LANES:LANES + 1], acc[1][LANES:LANES + 1]
        oa = acc[0][:LANES] * (1.0 / la)
        ob = acc[1][:LANES] * (1.0 / lb)
        if diff:
            ot = oa - lam * ob
            ms = jnp.mean(ot * ot, axis=0, keepdims=True)
            ot = ((ot * lax.rsqrt(ms + RMS_EPS)) * subg_ref[...]) * (1.0 - lambda_init)
            for row0, n, dst, stride in stores:
                assert stride == 1
                o_ref[dst:dst + n, :] = ot.T[row0:row0 + n].astype(BF16)
        else:
            o_rows = jnp.where(top, oa, ob).T
            lse_rows = jnp.where(top, m[0] + jnp.log2(la), m[1] + jnp.log2(lb)).T
            for row0, n, dst, stride in stores:
                rows(og_ref, g, dst, n, stride)[...] = o_rows[row0:row0 + n]
                rows(lse_ref, g, dst, n, stride)[...] = lse_rows[row0:row0 + n]

    def chunks_of(i):
        return {(dst + t * stride) // TQ for _, n, dst, stride in plan[i][3] for t in range(n)}

    nchunk = o_ref.shape[0] // TQ
    if diff:
        by_size = sorted(range(len(plan)), key=lambda i: len(tiles_of(i)))
        order = by_size[0::2] + by_size[1::2][::-1]
    else:
        order = []
        for c in range(nchunk):
            order += [i for i in range(len(plan)) if c in chunks_of(i) and i not in order]
    needs = {c: {i for i in range(len(plan)) if c in chunks_of(i)} for c in range(nchunk)}
    finished, merged = set(), set()

    def merge_ready():
        for c in range(nchunk):
            if c in merged or not needs[c] <= finished:
                continue
            merged.add(c)
            r0 = c * TQ
            lses = [lse_ref[g, r0:r0 + TQ, :] for g in range(nseg)]
            top_lse = functools.reduce(jnp.maximum, lses)
            ws = [jnp.exp2(l - top_lse) for l in lses]
            num = sum(w * og_ref[g, r0:r0 + TQ, :] for g, w in enumerate(ws))
            o_ref[r0:r0 + TQ, :] = (num * (1.0 / sum(ws))).astype(BF16)

    flat = [(i, n) for i in order for n in range(len(tiles_of(i)))]
    assert s_ref.shape[0] > SCORE_LEAD
    for t, (i, n) in enumerate(flat):
        slot[(i, n)] = t % s_ref.shape[0]
    issued = 0
    for t, (i, n) in enumerate(flat):
        while issued <= min(t + SCORE_LEAD, len(flat) - 1):
            scores(*flat[issued])
            issued += 1
        weigh(i, n)
        if n == len(tiles_of(i)) - 1:
            finish(i)
            finished.add(i)
            if not diff:
                merge_ready()
    assert diff or len(merged) == nchunk


def _attention(qt, k, vt, biases, plan, *, seq, diff, lambda_init=0.0, lamv=None, subg=None):
    groups, nblk, ntile, _, _ = qt.shape
    t, d = k.shape[1], k.shape[2]
    bsz = t // seq
    nq = seq // TQ
    nds = tuple(b.shape[0] for b in biases)
    assert bsz < 2 ** ZERO_SHIFT
    scratch = [pltpu.VMEM((SCORE_LEAD + 2, 2, TQ, TQ), F32)]
    if not diff:
        scratch += [pltpu.VMEM((groups, seq, LANES), F32), pltpu.VMEM((groups, seq, LANES), F32)]
    in_specs, args = [], []
    for g in range(groups):
        in_specs += [
            pl.BlockSpec((1, 1, nq, LANES, TQ), lambda j, b, g=g: (g, j, b, 0, 0)),
            pl.BlockSpec((1, seq, LANES), lambda j, b, g=g: (g, b, j)),
            pl.BlockSpec((1, 1, nq, LANES, TQ), lambda j, b, g=g: (g, j, b, 0, 0)),
            pl.BlockSpec((nds[g], 2, TQ, TQ), lambda j, b: (0, j, 0, 0)),
        ]
        args += [qt, k, vt, biases[g]]
    if diff:
        in_specs += [pl.BlockSpec(lamv.shape, lambda j, b: (0, 0)),
                     pl.BlockSpec((LANES, 1), lambda j, b: (0, 0))]
        args += [lamv, subg]
    return pl.pallas_call(
        functools.partial(_attn_kernel, plan=plan, nseg=groups, diff=diff, lambda_init=lambda_init),
        out_shape=jax.ShapeDtypeStruct((t, d), BF16),
        grid=(nblk, bsz),
        in_specs=in_specs,
        out_specs=pl.BlockSpec((seq, LANES), lambda j, b: (b, j)),
        scratch_shapes=scratch,
        compiler_params=_params("arbitrary", "arbitrary"),
        name="diff_attention" if diff else "dilated_attention",
    )(*args)


def _lru_kernel(x_ref, g_ref, sh_ref, sc_ref, win_ref, cw_ref, cb_ref, wx_ref, bx_ref,
                wa_ref, ba_ref, ap_ref, o_ref, uext_ref, h_ref):
    ts, d = x_ref.shape

    @pl.when(pl.program_id(1) == 0)
    def _():
        uext_ref[0:SUBLANES, :] = jnp.zeros((SUBLANES, d), F32)
        h_ref[...] = jnp.zeros_like(h_ref)

    h = _norm_mod(x_ref[...], g_ref[...], sh_ref[0], sc_ref[0]).astype(BF16)
    yu = jnp.dot(h, win_ref[...], preferred_element_type=F32)
    y = yu[:, :d]
    u = yu[:, d:]
    uext_ref[SUBLANES:SUBLANES + ts, :] = u
    conv = cb_ref[...] + u * cw_ref[0:1, :]
    for tap in range(1, CONV_WIDTH):
        conv = conv + uext_ref[SUBLANES - tap:SUBLANES - tap + ts, :] * cw_ref[tap:tap + 1, :]
    uext_ref[0:SUBLANES, :] = u[ts - SUBLANES:ts, :]
    uc = conv
    ucb = uc.astype(BF16)
    nblk = wx_ref.shape[0]
    bw = d // nblk
    gx = jnp.concatenate([jnp.dot(ucb[:, n * bw:(n + 1) * bw], wx_ref[n], preferred_element_type=F32)
                          for n in range(nblk)], axis=1) + bx_ref[...]
    ga = jnp.concatenate([jnp.dot(ucb[:, n * bw:(n + 1) * bw], wa_ref[n], preferred_element_type=F32)
                          for n in range(nblk)], axis=1) + ba_ref[...]
    gate_x = 1.0 / (1.0 + jnp.exp2(gx * -LOG2E))
    gate_a = 1.0 / (1.0 + jnp.exp2(ga * -LOG2E))
    z = -ap_ref[...]
    softplus = jnp.maximum(z, 0.0) + jnp.log1p(jnp.exp(-jnp.abs(z)))
    rate = -LRU_C * softplus
    log_a = gate_a * rate
    a = jnp.exp2(gate_a * (rate * LOG2E))
    t = -jnp.tanh(log_a) * (1.0 + a * a)
    b = jnp.where(t > 0.0, t * lax.rsqrt(t), 0.0) * (gate_x * uc)
    ngrp = ts // SUBLANES
    a = a.reshape(ngrp, SUBLANES, d)
    b = b.reshape(ngrp, SUBLANES, d)
    sub = lax.broadcasted_iota(jnp.int32, (ngrp, SUBLANES, d), 1)
    step = 1
    while step < SUBLANES:
        keep = sub >= step
        a_prev = pltpu.roll(a, step, 1)
        b_prev = pltpu.roll(b, step, 1)
        b = jnp.where(keep, a * b_prev, 0.0) + b
        a = jnp.where(keep, a * a_prev, a)
        step *= 2
    carry = h_ref[...]
    groups = []
    for r in range(ngrp):
        hg = a[r] * carry + b[r]
        groups.append(hg)
        carry = hg[SUBLANES - 1:SUBLANES, :]
    hs = jnp.concatenate(groups, axis=0)
    h_ref[...] = carry
    gelu = 0.5 * y * (1.0 + jnp.tanh(math.sqrt(2.0 / math.pi) * (y + 0.044715 * (y * y * y))))
    o_ref[...] = (hs * gelu).astype(BF16)


def _lru(x, mod, layer, gain, w_in, conv_w, conv_b, w_x, b_x, w_a, b_a, a_param, *, seq, ts):
    t, d = x.shape
    bsz = t // seq
    ns = seq // ts
    vec = lambda v: v.reshape(1, d)
    return pl.pallas_call(
        _lru_kernel,
        out_shape=jax.ShapeDtypeStruct((t, d), BF16),
        grid=(bsz, ns),
        in_specs=[pl.BlockSpec((ts, d), lambda b, s: (b * ns + s, 0)),
                  pl.BlockSpec((1, d), lambda b, s: (0, 0)),
                  pl.BlockSpec((1, 1, d), lambda b, s: (layer * bsz + b, 0, 0)),
                  pl.BlockSpec((1, 1, d), lambda b, s: (layer * bsz + b, 0, 1)),
                  _resident(w_in.shape),
                  pl.BlockSpec((CONV_WIDTH, d), lambda b, s: (0, 0)),
                  pl.BlockSpec((1, d), lambda b, s: (0, 0)),
                  _resident(w_x.shape),
                  pl.BlockSpec((1, d), lambda b, s: (0, 0)),
                  _resident(w_a.shape),
                  pl.BlockSpec((1, d), lambda b, s: (0, 0)),
                  pl.BlockSpec((1, d), lambda b, s: (0, 0))],
        out_specs=pl.BlockSpec((ts, d), lambda b, s: (b * ns + s, 0)),
        scratch_shapes=[pltpu.VMEM((ts + SUBLANES, d), F32), pltpu.VMEM((1, d), F32)],
        compiler_params=_params("arbitrary", "arbitrary"),
        name="rglru_mixer",
    )(x, vec(gain), mod, mod, w_in, conv_w, vec(conv_b), w_x, vec(b_x), w_a, vec(b_a), vec(a_param))


def _post_mlp_kernel(x_ref, y_ref, wo_ref, g1_ref, g_ref, sh_ref, sc_ref, g2_ref, w1_ref, w2_ref,
                     o_ref, *, ff_chunk):
    x1 = x_ref[...] + g1_ref[0] * jnp.dot(y_ref[...], wo_ref[...], preferred_element_type=F32)
    h = _norm_mod(x1, g_ref[...], sh_ref[0], sc_ref[0]).astype(BF16)
    dff = w1_ref.shape[1]
    acc = None
    for c in range(dff // ff_chunk):
        a = jnp.dot(h, w1_ref[:, c * ff_chunk:(c + 1) * ff_chunk], preferred_element_type=F32)
        a = jnp.maximum(a, 0.0)
        a = (a * a).astype(BF16)
        part = jnp.dot(a, w2_ref[c * ff_chunk:(c + 1) * ff_chunk, :], preferred_element_type=F32)
        acc = part if acc is None else acc + part
    o_ref[...] = x1 + g2_ref[0] * acc


def _post_mlp(x, y, mod, layer, w_o, gain, w1, w2, *, seq, tm, ff_chunk):
    t, d = x.shape
    bsz = t // seq
    row = lambda i: layer * bsz + (i * tm) // seq
    part = lambda p: pl.BlockSpec((1, 1, d), lambda i: (row(i), 0, p))
    return pl.pallas_call(
        functools.partial(_post_mlp_kernel, ff_chunk=ff_chunk),
        out_shape=jax.ShapeDtypeStruct((t, d), F32),
        grid=(t // tm,),
        in_specs=[pl.BlockSpec((tm, d), lambda i: (i, 0)),
                  pl.BlockSpec((tm, d), lambda i: (i, 0)),
                  _resident(w_o.shape),
                  part(2),
                  pl.BlockSpec((1, d), lambda i: (0, 0)),
                  part(3), part(4), part(5),
                  _resident(w1.shape), _resident(w2.shape)],
        out_specs=pl.BlockSpec((tm, d), lambda i: (i, 0)),
        compiler_params=_params("arbitrary"),
        name="outproj_mlp",
    )(x, y, w_o, mod, gain.reshape(1, d), mod, mod, mod, w1, w2)


def kernel(x, c, rel_bias, w_ada, b_ada, norm_mix, norm_mlp, mlp_w1, mlp_w2, da_w_qkv, da_w_o, da_q_gain, da_k_gain, da_lam_q1, da_lam_k1, da_lam_q2, da_lam_k2, da_sub_gain, lru_w_in, lru_conv_w, lru_conv_b, lru_w_x, lru_b_x, lru_w_a, lru_b_a, lru_a_param, lru_w_out, dil_w_qkv, dil_w_o, dil_q_gain, dil_k_gain):
    bsz, seq, d = x.shape
    depth = w_ada.shape[0]
    t = bsz * seq
    nq = seq // TQ
    assert seq % TQ == 0 and d % LANES == 0

    mod = _ada(c, w_ada, b_ada).reshape(depth * bsz, 1, 6 * d)

    ncols = rel_bias.shape[1]
    rb_ext = jnp.concatenate(
        [rel_bias.T.astype(F32), jnp.full((ncols, 1), NEG_INF, F32),
         jnp.zeros((ncols, BUCKET_PAD - NUM_BUCKETS - 1), F32)], axis=1)
    kinds = [i % N_MIXERS for i in range(depth)]
    da_bias = _bias_tiles(rb_ext, nq, 1, None, seq) if 0 in kinds else None
    dil_bias = None
    if 2 in kinds:
        dil_bias = [_bias_tiles(rb_ext, 2 if seq // r >= 2 * TQ else 1, r, w // r, seq // r)
                    for (w, r) in DIL_PATTERNS]

    q_scale = HEAD_DIM ** -0.5 * LOG2E
    big_tile = min(BIG_TILE_ROWS, seq)
    xf = x.reshape(t, d)
    for i in range(depth):
        kind, slot = kinds[i], i // N_MIXERS
        if kind == 0:
            w = da_w_qkv[slot]
            wq = w[:, :d].astype(BF16)[None]
            wk = w[:, d:2 * d].astype(BF16)[None]
            wv = w[:, 2 * d:].astype(BF16)[None]
            qg = (da_q_gain[slot] * q_scale).reshape(HEAD_DIM, 1)
            kg = jnp.tile(da_k_gain[slot], 2).reshape(1, LANES)
            qt, k, vt = _qkv(xf, mod, i, norm_mix[i], wq, wk, wv, qg, kg, seq=seq, tm=big_tile,
                             dilations=(1,))
            lambda_init = 0.8 - 0.6 * math.exp(-0.3 * i)
            lamv = jnp.stack([da_lam_q1[slot], da_lam_k1[slot], da_lam_q2[slot], da_lam_k2[slot]]).astype(F32)
            y = _attention(qt, k, vt, [da_bias], _diff_plan(nq), seq=seq, diff=True, lambda_init=lambda_init,
                           lamv=lamv, subg=da_sub_gain[slot].reshape(LANES, 1))
            w_o = da_w_o[slot]
        elif kind == 1:
            y = _lru(xf, mod, i, norm_mix[i], lru_w_in[slot].astype(BF16), lru_conv_w[slot],
                     lru_conv_b[slot], lru_w_x[slot].astype(BF16), lru_b_x[slot],
                     lru_w_a[slot].astype(BF16), lru_b_a[slot], lru_a_param[slot], seq=seq, ts=big_tile)
            w_o = lru_w_out[slot]
        else:
            w = dil_w_qkv[slot].reshape(d, len(DIL_PATTERNS), 3, d)
            wq = jnp.transpose(w[:, :, 0, :], (1, 2, 0)).astype(BF16)
            wk = jnp.transpose(w[:, :, 1, :], (1, 0, 2)).astype(BF16)
            wv = jnp.transpose(w[:, :, 2, :], (1, 2, 0)).astype(BF16)
            qg = (dil_q_gain[slot] * q_scale).reshape(HEAD_DIM, 1)
            kg = jnp.tile(dil_k_gain[slot], 2).reshape(1, LANES)
            qt, k, vt = _qkv(xf, mod, i, norm_mix[i], wq, wk, wv, qg, kg, seq=seq, tm=TQ,
                             dilations=tuple(r for _, r in DIL_PATTERNS))
            y = _attention(qt, k, vt, dil_bias, _dilated_plan(seq, DIL_PATTERNS), seq=seq, diff=False)
            w_o = dil_w_o[slot]
        xf = _post_mlp(xf, y, mod, i, w_o.astype(BF16), norm_mlp[i], mlp_w1[i].astype(BF16),
                       mlp_w2[i].astype(BF16), seq=seq, tm=big_tile, ff_chunk=MLP_FF_CHUNK)
    return xf.reshape(bsz, seq, d)
```
